```python
import jax, jax.numpy as jnp
from jax import lax
import numpy as np

D_MODEL = 1024
BATCH = 4
SEQ = 8192
DEPTH = 1
DEC_BATCH = 32
DEC_SEQ = 64
PAST_LEN = 4096

CHUNK = 64
QBLOCK = 128
D_FF = 2816
MLA_HEADS = 8
MLA_Q_LORA = 256
MLA_KV_LORA = 128
MLA_NOPE = 64
MLA_ROPE = 32
MLA_V = 64
SB_HEADS = 8
SB_DIM = 64
MIX_WIDTH = MLA_HEADS * MLA_V + SB_HEADS * SB_DIM
IN_WIDTH = MLA_Q_LORA + MLA_KV_LORA + MLA_ROPE + 3 * SB_HEADS * SB_DIM
MLA_SCALE = (MLA_NOPE + MLA_ROPE) ** -0.5
SB_SCALE = SB_DIM ** -0.5
ROPE_THETA = 10000.0
EPS = 1e-6

kernel_name = 'hybrid_mla_stickbreak_macaron_stream_step'


def rmsnorm(x, g):
    xf = x.astype(jnp.float32)
    y = xf * lax.rsqrt(jnp.mean(xf * xf, axis=-1, keepdims=True) + EPS)
    return (y * g.astype(jnp.float32)).astype(x.dtype)


def swiglu(x, w_gate, w_up, w_down):
    return (jax.nn.silu(x @ w_gate) * (x @ w_up)) @ w_down


def rope(x, pos):
    half = MLA_ROPE // 2
    inv_freq = ROPE_THETA ** (-jnp.arange(half, dtype=jnp.float32) / half)
    ang = pos.astype(jnp.float32)[:, None] * inv_freq[None, :]
    cos = jnp.cos(ang)[None, :, None, :]
    sin = jnp.sin(ang)[None, :, None, :]
    xf = x.astype(jnp.float32)
    x1, x2 = xf[..., :half], xf[..., half:]
    return jnp.concatenate([x1 * cos - x2 * sin, x1 * sin + x2 * cos], axis=-1).astype(x.dtype)


def sweep_query_blocks(fn, q_pos, *qs):
    nq = q_pos.shape[0]
    blk = QBLOCK if nq % QBLOCK == 0 else nq
    nb = nq // blk

    def split(a):
        return jnp.moveaxis(a.reshape(a.shape[0], nb, blk, *a.shape[2:]), 1, 0)

    out = lax.map(lambda a: fn(*a), (q_pos.reshape(nb, blk),) + tuple(split(a) for a in qs))
    out = jnp.moveaxis(out, 0, 1)
    return out.reshape(out.shape[0], nq, *out.shape[3:])


def mla_block(qp, q_lat, q_rope, latent, krope, k_pos):
    s = (jnp.einsum('bqhc,bkc->bhqk', q_lat, latent, preferred_element_type=jnp.float32)
         + jnp.einsum('bqhr,bkr->bhqk', q_rope, krope, preferred_element_type=jnp.float32)) * MLA_SCALE
    visible = (k_pos[None, :] // CHUNK) <= (qp[:, None] // CHUNK)
    s = jnp.where(visible[None, None], s, -jnp.inf)
    p = jax.nn.softmax(s, axis=-1).astype(latent.dtype)
    return jnp.einsum('bhqk,bkc->bqhc', p, latent)


def sb_block(qp, q, k, v, k_pos):
    z = jnp.einsum('bqhd,bkhd->bhqk', q, k, preferred_element_type=jnp.float32) * SB_SCALE
    before = (k_pos[None, :] < qp[:, None])[None, None]
    log_beta = jax.nn.log_sigmoid(z)
    log_rest = jnp.where(before, jax.nn.log_sigmoid(-z), 0.0)
    tail = lax.cumsum(log_rest, axis=3, reverse=True) - log_rest
    a = jnp.where(before, jnp.exp(log_beta + tail), 0.0).astype(v.dtype)
    return jnp.einsum('bhqk,bkhd->bqhd', a, v)


def token_mix(u, pos, past_latent, past_krope, past_k, past_v,
              w_in, g_q, w_uq, g_kv, w_uk, w_uv, g_mla_out, g_sb_out, w_out):
    B, S, _ = u.shape
    sbw = SB_HEADS * SB_DIM
    i1 = MLA_Q_LORA
    i2 = i1 + MLA_KV_LORA
    i3 = i2 + MLA_ROPE
    c_q, c_kv, k_r, q_sb, k_sb, v_sb = jnp.split(u @ w_in, [i1, i2, i3, i3 + sbw, i3 + 2 * sbw], axis=-1)
    q = jnp.einsum('bsc,chd->bshd', rmsnorm(c_q, g_q), w_uq)
    q_rope = rope(q[..., MLA_NOPE:], pos)
    q_lat = jnp.einsum('bshd,chd->bshc', q[..., :MLA_NOPE], w_uk)
    latent_new = rmsnorm(c_kv, g_kv)
    krope_new = rope(k_r[:, :, None, :], pos)[:, :, 0, :]
    q_sb = q_sb.reshape(B, S, SB_HEADS, SB_DIM)
    k_new = k_sb.reshape(B, S, SB_HEADS, SB_DIM)
    v_new = v_sb.reshape(B, S, SB_HEADS, SB_DIM)
    if past_latent is None:
        latent, krope, keys, vals, k_pos = latent_new, krope_new, k_new, v_new, pos
    else:
        n_past = past_latent.shape[1]
        latent = jnp.concatenate([past_latent, latent_new], axis=1)
        krope = jnp.concatenate([past_krope, krope_new], axis=1)
        keys = jnp.concatenate([past_k, k_new], axis=1)
        vals = jnp.concatenate([past_v, v_new], axis=1)
        k_pos = jnp.concatenate([jnp.arange(n_past, dtype=pos.dtype), pos])
    o_lat = sweep_query_blocks(lambda qp, ql, qr: mla_block(qp, ql, qr, latent, krope, k_pos), pos, q_lat, q_rope)
    o_mla = jnp.einsum('bshc,chd->bshd', o_lat, w_uv)
    o_sb = sweep_query_blocks(lambda qp, qq: sb_block(qp, qq, keys, vals, k_pos), pos, q_sb)
    o = jnp.concatenate([rmsnorm(o_mla, g_mla_out).reshape(B, S, MLA_HEADS * MLA_V),
                         rmsnorm(o_sb, g_sb_out).reshape(B, S, SB_HEADS * SB_DIM)], axis=-1)
    return o @ w_out, (latent_new, krope_new, k_new, v_new)


def encoder_layer(x, pos, past_latent, past_krope, past_k, past_v,
                  g_pre_ff1, w_gate1, w_up1, w_down1, g_post_ff1,
                  g_pre_mix, w_in, g_q, w_uq, g_kv, w_uk, w_uv, g_mla_out, g_sb_out, w_out, g_post_mix,
                  g_pre_ff2, w_gate2, w_up2, w_down2, g_post_ff2, g_final):
    h = x + 0.5 * rmsnorm(swiglu(rmsnorm(x, g_pre_ff1), w_gate1, w_up1, w_down1), g_post_ff1)
    m, rows = token_mix(rmsnorm(h, g_pre_mix), pos, past_latent, past_krope, past_k, past_v,
                        w_in, g_q, w_uq, g_kv, w_uk, w_uv, g_mla_out, g_sb_out, w_out)
    h = h + rmsnorm(m, g_post_mix)
    h = h + 0.5 * rmsnorm(swiglu(rmsnorm(h, g_pre_ff2), w_gate2, w_up2, w_down2), g_post_ff2)
    return rmsnorm(h, g_final), rows


def setup_inputs(seed: int = 0) -> dict:
    key = jax.random.key(seed)
    ks = iter(jax.random.split(key, 40))
    L = DEPTH

    def nrm(shape, fan_in):
        return jax.random.normal(next(ks), shape, jnp.float32) * fan_in ** -0.5

    def gain(shape):
        return 1.0 + 0.05 * jax.random.normal(next(ks), shape, jnp.float32)

    def unit(shape):
        return jax.random.normal(next(ks), shape, jnp.float32)

    return {
        'x_prompt': unit((BATCH, SEQ, D_MODEL)),
        'x_sample': unit((DEC_BATCH, DEC_SEQ, D_MODEL)),
        'cache_mla_latent': unit((L, DEC_BATCH, PAST_LEN, MLA_KV_LORA)),
        'cache_mla_krope': unit((L, DEC_BATCH, PAST_LEN, MLA_ROPE)),
        'cache_sb_k': unit((L, DEC_BATCH, PAST_LEN, SB_HEADS, SB_DIM)),
        'cache_sb_v': unit((L, DEC_BATCH, PAST_LEN, SB_HEADS, SB_DIM)),
        'g_pre_ff1': gain((L, D_MODEL)),
        'w_gate1': nrm((L, D_MODEL, D_FF), D_MODEL),
        'w_up1': nrm((L, D_MODEL, D_FF), D_MODEL),
        'w_down1': nrm((L, D_FF, D_MODEL), D_FF),
        'g_post_ff1': gain((L, D_MODEL)),
        'g_pre_mix': gain((L, D_MODEL)),
        'w_in': nrm((L, D_MODEL, IN_WIDTH), D_MODEL),
        'g_q': gain((L, MLA_Q_LORA)),
        'w_uq': nrm((L, MLA_Q_LORA, MLA_HEADS, MLA_NOPE + MLA_ROPE), MLA_Q_LORA),
        'g_kv': gain((L, MLA_KV_LORA)),
        'w_uk': nrm((L, MLA_KV_LORA, MLA_HEADS, MLA_NOPE), MLA_KV_LORA),
        'w_uv': nrm((L, MLA_KV_LORA, MLA_HEADS, MLA_V), MLA_KV_LORA),
        'g_mla_out': gain((L, MLA_HEADS, MLA_V)),
        'g_sb_out': gain((L, SB_HEADS, SB_DIM)),
        'w_out': nrm((L, MIX_WIDTH, D_MODEL), MIX_WIDTH),
        'g_post_mix': gain((L, D_MODEL)),
        'g_pre_ff2': gain((L, D_MODEL)),
        'w_gate2': nrm((L, D_MODEL, D_FF), D_MODEL),
        'w_up2': nrm((L, D_MODEL, D_FF), D_MODEL),
        'w_down2': nrm((L, D_FF, D_MODEL), D_FF),
        'g_post_ff2': gain((L, D_MODEL)),
        'g_final': gain((L, D_MODEL)),
    }


def reference(x_prompt, x_sample, cache_mla_latent, cache_mla_krope, cache_sb_k, cache_sb_v,
              g_pre_ff1, w_gate1, w_up1, w_down1, g_post_ff1,
              g_pre_mix, w_in, g_q, w_uq, g_kv, w_uk, w_uv, g_mla_out, g_sb_out, w_out, g_post_mix,
              g_pre_ff2, w_gate2, w_up2, w_down2, g_post_ff2, g_final):
    weights = (g_pre_ff1, w_gate1, w_up1, w_down1, g_post_ff1,
               g_pre_mix, w_in, g_q, w_uq, g_kv, w_uk, w_uv, g_mla_out, g_sb_out, w_out, g_post_mix,
               g_pre_ff2, w_gate2, w_up2, w_down2, g_post_ff2, g_final)
    pos_p = jnp.arange(x_prompt.shape[1], dtype=jnp.int32)
    pos_s = cache_mla_latent.shape[2] + jnp.arange(x_sample.shape[1], dtype=jnp.int32)
    hp, hs = x_prompt, x_sample
    rows_p, rows_s = [], []
    for l in range(DEPTH):
        wl = [w[l] for w in weights]
        hp, rp = encoder_layer(hp, pos_p, None, None, None, None, *wl)
        hs, rs = encoder_layer(hs, pos_s, cache_mla_latent[l], cache_mla_krope[l],
                               cache_sb_k[l], cache_sb_v[l], *wl)
        rows_p.append(rp)
        rows_s.append(rs)

    def stack(rows, i):
        return jnp.stack([r[i] for r in rows])

    return (hp, hs,
            stack(rows_p, 0), stack(rows_p, 1), stack(rows_p, 2), stack(rows_p, 3),
            stack(rows_s, 0), stack(rows_s, 1), stack(rows_s, 2), stack(rows_s, 3))
```

```python
import functools

import jax
import jax.numpy as jnp
from jax import lax
from jax.experimental import pallas as pl
from jax.experimental.pallas import tpu as pltpu

EPS = 1e-6
CHUNK = 64
ROPE_THETA = 10000.0
LANES = 128
F32 = jnp.float32
BF16 = jnp.bfloat16
VMEM_LIMIT = 56 * 1024 * 1024


def _dot(a, b):
    return jnp.dot(a, b, preferred_element_type=F32)


def _dot_nt(a, b):
    return lax.dot_general(a, b, (((1,), (1,)), ((), ())), preferred_element_type=F32)


def _rms(x, g):
    return x * lax.rsqrt(jnp.mean(x * x, axis=-1, keepdims=True) + EPS) * g


def _params(*sem):
    return pltpu.CompilerParams(dimension_semantics=sem, vmem_limit_bytes=VMEM_LIMIT)


def _ffn_kernel(x_ref, gpre_ref, wg_ref, wu_ref, wd_ref, gpost_ref, gfin_ref, o_ref, *, final, fc):
    x = x_ref[...]
    xn = _rms(x, gpre_ref[...]).astype(BF16)
    d_ff = wg_ref.shape[1]
    acc = jnp.zeros(x.shape, F32)
    for c in range(0, d_ff, fc):
        g = _dot(xn, wg_ref[:, c:c + fc])
        u = _dot(xn, wu_ref[:, c:c + fc])
        a = (g * jax.nn.sigmoid(g)) * u
        acc = acc + _dot(a.astype(BF16), wd_ref[c:c + fc, :])
    h = x + 0.5 * _rms(acc, gpost_ref[...])
    if final:
        h = _rms(h, gfin_ref[...])
    o_ref[...] = h


def _ffn(x, g_pre, wg, wu, wd, g_post, g_fin, *, final, tm):
    t, d = x.shape
    d_ff = wg.shape[1]
    fc = d_ff // 2 if (d_ff // 2) % LANES == 0 else d_ff
    const = lambda i: (0, 0)
    return pl.pallas_call(
        functools.partial(_ffn_kernel, final=final, fc=fc),
        grid=(t // tm,),
        in_specs=[
            pl.BlockSpec((tm, d), lambda i: (i, 0)),
            pl.BlockSpec((1, d), const),
            pl.BlockSpec((d, d_ff), const, pipeline_mode=pl.Buffered(1)),
            pl.BlockSpec((d, d_ff), const, pipeline_mode=pl.Buffered(1)),
            pl.BlockSpec((d_ff, d), const, pipeline_mode=pl.Buffered(1)),
            pl.BlockSpec((1, d), const),
            pl.BlockSpec((1, d), const),
        ],
        out_specs=pl.BlockSpec((tm, d), lambda i: (i, 0)),
        out_shape=jax.ShapeDtypeStruct((t, d), F32),
        compiler_params=_params("parallel"),
        name="ffn_final" if final else "ffn",
    )(x, g_pre, wg, wu, wd, g_post, g_fin)


def _mix_in_kernel(h_ref, gpre_ref, win_ref, gq_ref, wuq_ref, wuk_ref, gkv_ref, cos_ref, sin_ref,
                   lat_ref, kr_ref, kn_ref, vn_ref, qcat_ref, kcat_ref, qsb_ref, ksb_ref, vsb_ref,
                   *, heads, q_lora, kv_lora, rope_dim, sbw, mla_scale, sb_scale):
    u = _rms(h_ref[...], gpre_ref[...]).astype(BF16)
    proj = _dot(u, win_ref[...])
    cosp = cos_ref[...]
    sinp = sin_ref[...]
    o1 = q_lora
    o2 = o1 + kv_lora
    o3 = o2 + LANES

    def rope(grp):
        return grp * cosp + pltpu.roll(grp, LANES - rope_dim, 1) * sinp

    latent = _rms(proj[:, o1:o2], gkv_ref[...])
    krope = rope(proj[:, o2:o3])
    lat_ref[...] = latent
    kr_ref[...] = krope[:, :rope_dim]
    k_sb = proj[:, o3 + sbw:o3 + 2 * sbw]
    v_sb = proj[:, o3 + 2 * sbw:o3 + 3 * sbw]
    kn_ref[...] = k_sb
    vn_ref[...] = v_sb
    ksb_ref[...] = k_sb.astype(BF16)
    vsb_ref[...] = v_sb.astype(BF16)
    qsb_ref[...] = (proj[:, o3:o3 + sbw] * sb_scale).astype(BF16)
    kcat_ref[...] = jnp.concatenate([latent, krope], axis=1).astype(BF16)

    c_q = _rms(proj[:, :o1], gq_ref[...]).astype(BF16)
    q = _dot(c_q, wuq_ref[...])
    n_nope = wuk_ref.shape[0]
    q_lat = _dot(q[:, :n_nope].astype(BF16), wuk_ref[...])
    for hd in range(heads):
        ql = q_lat[:, hd * kv_lora:(hd + 1) * kv_lora] * mla_scale
        qr = rope(q[:, n_nope + hd * LANES:n_nope + (hd + 1) * LANES]) * mla_scale
        qcat_ref[hd] = jnp.concatenate([ql, qr], axis=1).astype(BF16)


def _mix_in(h, g_pre, w_in_ext, g_q, w_uq_ext, w_uk_bd, g_kv, cosp, sinp, *, tm, dims):
    t, d = h.shape
    heads, q_lora, kv_lora, rope_dim, sbw, mla_scale, sb_scale = dims
    n_tab = cosp.shape[0] // tm
    const = lambda i: (0, 0)
    row = lambda i: (i, 0)
    tab = lambda i: (i % n_tab, 0)
    out_shape = (
        jax.ShapeDtypeStruct((t, kv_lora), F32),
        jax.ShapeDtypeStruct((t, rope_dim), F32),
        jax.ShapeDtypeStruct((t, sbw), F32),
        jax.ShapeDtypeStruct((t, sbw), F32),
        jax.ShapeDtypeStruct((heads, t, 2 * LANES), BF16),
        jax.ShapeDtypeStruct((t, 2 * LANES), BF16),
        jax.ShapeDtypeStruct((t, sbw), BF16),
        jax.ShapeDtypeStruct((t, sbw), BF16),
        jax.ShapeDtypeStruct((t, sbw), BF16),
    )
    out_specs = (
        pl.BlockSpec((tm, kv_lora), row),
        pl.BlockSpec((tm, rope_dim), row),
        pl.BlockSpec((tm, sbw), row),
        pl.BlockSpec((tm, sbw), row),
        pl.BlockSpec((heads, tm, 2 * LANES), lambda i: (0, i, 0)),
        pl.BlockSpec((tm, 2 * LANES), row),
        pl.BlockSpec((tm, sbw), row),
        pl.BlockSpec((tm, sbw), row),
        pl.BlockSpec((tm, sbw), row),
    )
    return pl.pallas_call(
        functools.partial(_mix_in_kernel, heads=heads, q_lora=q_lora, kv_lora=kv_lora,
                          rope_dim=rope_dim, sbw=sbw, mla_scale=mla_scale, sb_scale=sb_scale),
        grid=(t // tm,),
        in_specs=[
            pl.BlockSpec((tm, d), row),
            pl.BlockSpec((1, d), const),
            pl.BlockSpec(w_in_ext.shape, const),
            pl.BlockSpec((1, q_lora), const),
            pl.BlockSpec(w_uq_ext.shape, const),
            pl.BlockSpec(w_uk_bd.shape, const),
            pl.BlockSpec((1, kv_lora), const),
            pl.BlockSpec((tm, LANES), tab),
            pl.BlockSpec((tm, LANES), tab),
        ],
        out_specs=out_specs,
        out_shape=out_shape,
        compiler_params=_params("parallel"),
        name="mix_in",
    )(h, g_pre, w_in_ext, g_q, w_uq_ext, w_uk_bd, g_kv, cosp, sinp)


def _mla_update(q, kblk, vblk, m_ref, l_ref, acc_ref, visible):
    s = sum(_dot_nt(qq, kk) for qq, kk in zip(q, kblk))
    if visible is not None:
        s = jnp.where(visible, s, -jnp.inf)
    m_prev = m_ref[...]
    m_new = jnp.maximum(m_prev, jnp.max(s, axis=1, keepdims=True))
    alpha = jnp.exp(m_prev - m_new)
    p = jnp.exp(s - m_new)
    l_ref[...] = alpha * l_ref[...] + jnp.sum(p, axis=1, keepdims=True)
    acc_ref[...] = alpha * acc_ref[...] + _dot(p.astype(BF16), vblk)
    m_ref[...] = m_new


def _mla_finish(l_ref, acc_ref, wuv_ref, g_ref, o_ref, heads, tq):
    o_lat = (acc_ref[...] / l_ref[...]).astype(BF16)
    v_dim = wuv_ref.shape[2]
    for hd in range(heads):
        o = _dot(o_lat[hd * tq:(hd + 1) * tq], wuv_ref[hd])
        o_ref[:, hd * v_dim:(hd + 1) * v_dim] = _rms(o, g_ref[:, hd * v_dim:(hd + 1) * v_dim]).astype(BF16)


def _mla_init(m_ref, l_ref, acc_ref):
    m_ref[...] = jnp.full(m_ref.shape, -jnp.inf, F32)
    l_ref[...] = jnp.zeros(l_ref.shape, F32)
    acc_ref[...] = jnp.zeros(acc_ref.shape, F32)


def _mla_prompt_kernel(q_ref, k_ref, wuv_ref, g_ref, o_ref, m_ref, l_ref, acc_ref, *, heads, tq, tk, c_dim):
    j = pl.program_id(1)
    q0 = j * tq
    q = q_ref[...].reshape(heads * tq, q_ref.shape[2])
    _mla_init(m_ref, l_ref, acc_ref)
    n_full = q0 // tk

    def full_block(kb, carry):
        kblk = k_ref[pl.ds(pl.multiple_of(kb * tk, tk), tk), :]
        _mla_update([q], [kblk], kblk[:, :c_dim], m_ref, l_ref, acc_ref, None)
        return carry

    lax.fori_loop(0, n_full, full_block, 0)
    k0 = pl.multiple_of(n_full * tk, tk)
    kblk = k_ref[pl.ds(k0, tk), :]
    qpos = q0 + lax.broadcasted_iota(jnp.int32, (heads, tq, tk), 1).reshape(heads * tq, tk)
    kpos = k0 + lax.broadcasted_iota(jnp.int32, (heads * tq, tk), 1)
    visible = (kpos // CHUNK) <= (qpos // CHUNK)
    _mla_update([q], [kblk], kblk[:, :c_dim], m_ref, l_ref, acc_ref, visible)
    _mla_finish(l_ref, acc_ref, wuv_ref, g_ref, o_ref, heads, tq)


def _mla_prompt(qcat, kcat, w_uv_h, g_out, *, batch, seq, tq, tk):
    heads, t, dk = qcat.shape
    c_dim, v_dim = w_uv_h.shape[1], w_uv_h.shape[2]
    assert tk % tq == 0 and tq % CHUNK == 0 and seq % tk == 0
    nq = seq // tq
    rows = heads * tq
    return pl.pallas_call(
        functools.partial(_mla_prompt_kernel, heads=heads, tq=tq, tk=tk, c_dim=c_dim),
        grid=(batch, nq),
        in_specs=[
            pl.BlockSpec((heads, tq, dk), lambda b, j: (0, b * nq + j, 0)),
            pl.BlockSpec((seq, dk), lambda b, j: (b, 0)),
            pl.BlockSpec(w_uv_h.shape, lambda b, j: (0, 0, 0)),
            pl.BlockSpec(g_out.shape, lambda b, j: (0, 0)),
        ],
        out_specs=pl.BlockSpec((tq, heads * v_dim), lambda b, j: (b * nq + j, 0)),
        out_shape=jax.ShapeDtypeStruct((t, heads * v_dim), BF16),
        scratch_shapes=[pltpu.VMEM((rows, 1), F32), pltpu.VMEM((rows, 1), F32), pltpu.VMEM((rows, c_dim), F32)],
        compiler_params=_params("parallel", "arbitrary"),
        name="mla_prompt",
    )(qcat, kcat, w_uv_h, g_out)


def _mla_sample_kernel(q_ref, lat_ref, kr_ref, knew_ref, wuv_ref, g_ref, o_ref, m_ref, l_ref, acc_ref,
                       *, heads, tq, tk, c_dim, rope_dim):
    q = q_ref[...].reshape(heads * tq, q_ref.shape[2])
    q_lat = q[:, :c_dim]
    q_rope = q[:, c_dim:c_dim + rope_dim]
    _mla_init(m_ref, l_ref, acc_ref)
    n_past = lat_ref.shape[0]

    def past_block(kb, carry):
        k0 = pl.multiple_of(kb * tk, tk)
        lat = lat_ref[pl.ds(k0, tk), :].astype(BF16)
        kr = kr_ref[pl.ds(k0, tk), :].astype(BF16)
        _mla_update([q_lat, q_rope], [lat, kr], lat, m_ref, l_ref, acc_ref, None)
        return carry

    lax.fori_loop(0, n_past // tk, past_block, 0)
    knew = knew_ref[...]
    _mla_update([q], [knew], knew[:, :c_dim], m_ref, l_ref, acc_ref, None)
    _mla_finish(l_ref, acc_ref, wuv_ref, g_ref, o_ref, heads, tq)


def _mla_sample(qcat, cache_lat, cache_kr, kcat, w_uv_h, g_out, *, tk):
    heads, t, dk = qcat.shape
    batch, n_past, c_dim = cache_lat.shape
    rope_dim = cache_kr.shape[2]
    tq = t // batch
    v_dim = w_uv_h.shape[2]
    assert n_past % CHUNK == 0 and tq <= CHUNK and n_past % tk == 0
    rows = heads * tq
    return pl.pallas_call(
        functools.partial(_mla_sample_kernel, heads=heads, tq=tq, tk=tk, c_dim=c_dim, rope_dim=rope_dim),
        grid=(batch,),
        in_specs=[
            pl.BlockSpec((heads, tq, dk), lambda b: (0, b, 0)),
            pl.BlockSpec((None, n_past, c_dim), lambda b: (b, 0, 0)),
            pl.BlockSpec((None, n_past, rope_dim), lambda b: (b, 0, 0)),
            pl.BlockSpec((tq, dk), lambda b: (b, 0)),
            pl.BlockSpec(w_uv_h.shape, lambda b: (0, 0, 0)),
            pl.BlockSpec(g_out.shape, lambda b: (0, 0)),
        ],
        out_specs=pl.BlockSpec((tq, heads * v_dim), lambda b: (b, 0)),
        out_shape=jax.ShapeDtypeStruct((t, heads * v_dim), BF16),
        scratch_shapes=[pltpu.VMEM((rows, 1), F32), pltpu.VMEM((rows, 1), F32), pltpu.VMEM((rows, c_dim), F32)],
        compiler_params=_params("parallel"),
        name="mla_sample",
    )(qcat, cache_lat, cache_kr, kcat, w_uv_h, g_out)


def _softplus(z):
    return jnp.maximum(z, 0.0) + jnp.log1p(jnp.exp(-jnp.abs(z)))


def _sb_update(qh, kblk, vblk, tri, carry, acc, before):
    z = _dot_nt(qh, kblk)
    sp = _softplus(z)
    if before is not None:
        sp = jnp.where(before, sp, 0.0)
    hi = sp.astype(BF16)
    lo = (sp - hi.astype(F32)).astype(BF16)
    c = _dot(hi, tri) + _dot(lo, tri) + carry
    a = jnp.exp(z - c)
    if before is not None:
        a = jnp.where(before, a, 0.0)
    acc = acc + _dot(a.astype(BF16), vblk)
    return c[:, 0:1], acc


def _head_masks(width, head_dim):
    lane = lax.broadcasted_iota(jnp.int32, (1, width), 1)
    return [(lane >= h * head_dim) & (lane < (h + 1) * head_dim) for h in range(width // head_dim)]


def _sb_finish(accs, masks, g_ref, o_ref, head_dim):
    o = accs[0]
    for acc, msk in zip(accs[1:], masks[1:]):
        o = jnp.where(msk, acc, o)
    sq = o * o
    inv = jnp.zeros_like(o)
    for msk in masks:
        ms = jnp.sum(jnp.where(msk, sq, 0.0), axis=1, keepdims=True) / head_dim
        inv = jnp.where(msk, lax.rsqrt(ms + EPS), inv)
    o_ref[...] = (o * inv * g_ref[...]).astype(BF16)


def _sb_prompt_kernel(q_ref, k_ref, v_ref, tri_ref, g_ref, o_ref, *, tq, head_dim):
    j = pl.program_id(2)
    q = q_ref[...]
    width = q.shape[1]
    masks = _head_masks(width, head_dim)
    qhs = [jnp.where(m, q, jnp.zeros_like(q)) for m in masks]
    tri = tri_ref[...]
    row = lax.broadcasted_iota(jnp.int32, (tq, tq), 0)
    col = lax.broadcasted_iota(jnp.int32, (tq, tq), 1)
    before = col < row
    k0 = pl.multiple_of(j * tq, tq)
    kblk = k_ref[pl.ds(k0, tq), :]
    vblk = v_ref[pl.ds(k0, tq), :]
    state = []
    for qh in qhs:
        state += list(_sb_update(qh, kblk, vblk, tri, jnp.zeros((tq, 1), F32), jnp.zeros((tq, width), F32), before))

    def earlier_block(i, state):
        kb = j - 1 - i
        k0 = pl.multiple_of(kb * tq, tq)
        kblk = k_ref[pl.ds(k0, tq), :]
        vblk = v_ref[pl.ds(k0, tq), :]
        out = []
        for n, qh in enumerate(qhs):
            out += list(_sb_update(qh, kblk, vblk, tri, state[2 * n], state[2 * n + 1], None))
        return tuple(out)

    state = lax.fori_loop(0, j, earlier_block, tuple(state))
    _sb_finish(state[1::2], masks, g_ref, o_ref, head_dim)


def _tri(n):
    r = lax.broadcasted_iota(jnp.int32, (n, n), 0)
    c = lax.broadcasted_iota(jnp.int32, (n, n), 1)
    return (r >= c).astype(BF16)


def _sb_prompt(q_sb, k_sb, v_sb, g_out, *, batch, seq, tq, head_dim):
    t, sbw = q_sb.shape
    nq = seq // tq
    n_grp = sbw // LANES
    return pl.pallas_call(
        functools.partial(_sb_prompt_kernel, tq=tq, head_dim=head_dim),
        grid=(batch, n_grp, nq),
        in_specs=[
            pl.BlockSpec((tq, LANES), lambda b, p, j: (b * nq + j, p)),
            pl.BlockSpec((seq, LANES), lambda b, p, j: (b, p)),
            pl.BlockSpec((seq, LANES), lambda b, p, j: (b, p)),
            pl.BlockSpec((tq, tq), lambda b, p, j: (0, 0)),
            pl.BlockSpec((1, LANES), lambda b, p, j: (0, p)),
        ],
        out_specs=pl.BlockSpec((tq, LANES), lambda b, p, j: (b * nq + j, p)),
        out_shape=jax.ShapeDtypeStruct((t, sbw), BF16),
        compiler_params=_params("parallel", "parallel", "arbitrary"),
        name="sb_prompt",
    )(q_sb, k_sb, v_sb, _tri(tq), g_out)


def _sb_sample_kernel(q_ref, kc_ref, vc_ref, kn_ref, vn_ref, tri_ref, g_ref, o_ref, *, tk, head_dim):
    q = q_ref[...]
    tq, width = q.shape
    masks = _head_masks(width, head_dim)
    qhs = [jnp.where(m, q, jnp.zeros_like(q)) for m in masks]
    tri = tri_ref[...]
    row = lax.broadcasted_iota(jnp.int32, (tq, tq), 0)
    col = lax.broadcasted_iota(jnp.int32, (tq, tq), 1)
    before = col < row
    kn = kn_ref[...]
    vn = vn_ref[...]
    state = []
    for qh in qhs:
        state += list(_sb_update(qh, kn, vn, (row >= col).astype(BF16), jnp.zeros((tq, 1), F32),
                                 jnp.zeros((tq, width), F32), before))
    n_blk = kc_ref.shape[0] // tk

    def earlier_block(i, state):
        k0 = pl.multiple_of((n_blk - 1 - i) * tk, tk)
        kblk = kc_ref[pl.ds(k0, tk), :].astype(BF16)
        vblk = vc_ref[pl.ds(k0, tk), :].astype(BF16)
        out = []
        for n, qh in enumerate(qhs):
            out += list(_sb_update(qh, kblk, vblk, tri, state[2 * n], state[2 * n + 1], None))
        return tuple(out)

    state = lax.fori_loop(0, n_blk, earlier_block, tuple(state))
    _sb_finish(state[1::2], masks, g_ref, o_ref, head_dim)


def _sb_sample(q_sb, cache_k, cache_v, k_sb, v_sb, g_out, *, tk, head_dim):
    t, sbw = q_sb.shape
    batch, n_past, _ = cache_k.shape
    tq = t // batch
    n_grp = sbw // LANES
    assert n_past % tk == 0 and tq <= tk
    new = pl.BlockSpec((tq, LANES), lambda b, p: (b, p))
    past = pl.BlockSpec((None, n_past, LANES), lambda b, p: (b, 0, p))
    return pl.pallas_call(
        functools.partial(_sb_sample_kernel, tk=tk, head_dim=head_dim),
        grid=(batch, n_grp),
        in_specs=[new, past, past, new, new,
                  pl.BlockSpec((tk, tk), lambda b, p: (0, 0)),
                  pl.BlockSpec((1, LANES), lambda b, p: (0, p))],
        out_specs=new,
        out_shape=jax.ShapeDtypeStruct((t, sbw), BF16),
        compiler_params=_params("parallel", "parallel"),
        name="sb_sample",
    )(q_sb, cache_k, cache_v, k_sb, v_sb, _tri(tk), g_out)


def _out_proj_kernel(oa_ref, ob_ref, h_ref, wa_ref, wb_ref, g_ref, o_ref):
    m = _dot(oa_ref[...], wa_ref[...]) + _dot(ob_ref[...], wb_ref[...])
    o_ref[...] = h_ref[...] + _rms(m, g_ref[...])


def _out_proj(o_a, o_b, h, w_a, w_b, g_post, *, tm):
    t, d = h.shape
    row = lambda i: (i, 0)
    const = lambda i: (0, 0)
    return pl.pallas_call(
        _out_proj_kernel,
        grid=(t // tm,),
        in_specs=[
            pl.BlockSpec((tm, o_a.shape[1]), row),
            pl.BlockSpec((tm, o_b.shape[1]), row),
            pl.BlockSpec((tm, d), row),
            pl.BlockSpec(w_a.shape, const),
            pl.BlockSpec(w_b.shape, const),
            pl.BlockSpec((1, d), const),
        ],
        out_specs=pl.BlockSpec((tm, d), row),
        out_shape=jax.ShapeDtypeStruct((t, d), F32),
        compiler_params=_params("parallel"),
        name="out_proj",
    )(o_a, o_b, h, w_a, w_b, g_post)


def _rot_cols(w):
    half = w.shape[-1] // 2
    return jnp.concatenate([-w[..., half:], w[..., :half]], axis=-1)


def _rope_tables(pos, rope_dim, rows):
    half = rope_dim // 2
    inv_freq = ROPE_THETA ** (-jnp.arange(half, dtype=F32) / half)
    ang = pos.astype(F32)[:, None] * inv_freq[None, :]
    pad = jnp.zeros((pos.shape[0], LANES - rope_dim), F32)
    cosp = jnp.concatenate([jnp.cos(ang), jnp.cos(ang), pad], axis=1)
    sinp = jnp.concatenate([jnp.sin(ang), jnp.sin(ang), pad], axis=1)
    reps = max(1, rows // pos.shape[0])
    return jnp.tile(cosp, (reps, 1)), jnp.tile(sinp, (reps, 1))


def kernel(x_prompt, x_sample, cache_mla_latent, cache_mla_krope, cache_sb_k, cache_sb_v, g_pre_ff1, w_gate1, w_up1, w_down1, g_post_ff1, g_pre_mix, w_in, g_q, w_uq, g_kv, w_uk, w_uv, g_mla_out, g_sb_out, w_out, g_post_mix, g_pre_ff2, w_gate2, w_up2, w_down2, g_post_ff2, g_final):
    depth = w_in.shape[0]
    assert depth == 1
    batch, seq, d = x_prompt.shape
    dec_batch, dec_seq, _ = x_sample.shape
    n_past = cache_mla_latent.shape[2]
    _, q_lora, heads, qk_dim = w_uq.shape
    kv_lora, _, nope = w_uk.shape[1:]
    rope_dim = qk_dim - nope
    v_dim = w_uv.shape[3]
    sb_heads, sb_dim = g_sb_out.shape[1:]
    sbw = sb_heads * sb_dim
    mla_scale = float(qk_dim) ** -0.5
    sb_scale = float(sb_dim) ** -0.5
    assert kv_lora == LANES and 2 * rope_dim <= LANES and LANES % sb_dim == 0
    dims = (heads, q_lora, kv_lora, rope_dim, sbw, mla_scale, sb_scale)

    row = lambda g: g.reshape(1, -1).astype(F32)
    l = 0
    wi = w_in[l]
    i1 = q_lora + kv_lora
    i2 = i1 + rope_dim
    w_kr = wi[:, i1:i2]
    w_in_ext = jnp.concatenate(
        [wi[:, :i1], w_kr, _rot_cols(w_kr), jnp.zeros((d, LANES - 2 * rope_dim), F32), wi[:, i2:]], axis=1).astype(BF16)
    wq = w_uq[l]
    wq_rope = wq[:, :, nope:]
    wq_grp = jnp.concatenate(
        [wq_rope, _rot_cols(wq_rope), jnp.zeros((q_lora, heads, LANES - 2 * rope_dim), F32)], axis=2)
    w_uq_ext = jnp.concatenate(
        [wq[:, :, :nope].reshape(q_lora, heads * nope), wq_grp.reshape(q_lora, heads * LANES)], axis=1).astype(BF16)
    wk = jnp.transpose(w_uk[l], (1, 2, 0))
    eye = jnp.eye(heads, dtype=F32)
    w_uk_bd = (wk[:, :, None, :] * eye[:, None, :, None]).reshape(heads * nope, heads * kv_lora).astype(BF16)
    w_uv_h = jnp.transpose(w_uv[l], (1, 0, 2)).astype(BF16)
    w_out_a = w_out[l][:heads * v_dim].astype(BF16)
    w_out_b = w_out[l][heads * v_dim:].astype(BF16)
    ffn1 = (row(g_pre_ff1[l]), w_gate1[l].astype(BF16), w_up1[l].astype(BF16), w_down1[l].astype(BF16),
            row(g_post_ff1[l]), row(g_final[l]))
    ffn2 = (row(g_pre_ff2[l]), w_gate2[l].astype(BF16), w_up2[l].astype(BF16), w_down2[l].astype(BF16),
            row(g_post_ff2[l]), row(g_final[l]))
    g_mla = row(g_mla_out[l])
    g_sb = row(g_sb_out[l])

    def layer(x, pos, tm, attend):
        t = x.shape[0]
        h = _ffn(x, *ffn1, final=False, tm=tm)
        cosp, sinp = _rope_tables(pos, rope_dim, tm)
        lat, kr, kn, vn, qcat, kcat, q_sb, k_sb, v_sb = _mix_in(
            h, row(g_pre_mix[l]), w_in_ext, row(g_q[l]), w_uq_ext, w_uk_bd, row(g_kv[l]), cosp, sinp,
            tm=tm, dims=dims)
        o_mla, o_sb = attend(qcat, kcat, q_sb, k_sb, v_sb)
        h = _out_proj(o_mla, o_sb, h, w_out_a, w_out_b, row(g_post_mix[l]), tm=tm)
        y = _ffn(h, *ffn2, final=True, tm=tm)
        return y, (lat, kr, kn, vn)

    def attend_prompt(qcat, kcat, q_sb, k_sb, v_sb):
        o_mla = _mla_prompt(qcat, kcat, w_uv_h, g_mla, batch=batch, seq=seq, tq=128, tk=256)
        o_sb = _sb_prompt(q_sb, k_sb, v_sb, g_sb, batch=batch, seq=seq, tq=256, head_dim=sb_dim)
        return o_mla, o_sb

    def attend_sample(qcat, kcat, q_sb, k_sb, v_sb):
        o_mla = _mla_sample(qcat, cache_mla_latent[l], cache_mla_krope[l], kcat, w_uv_h, g_mla, tk=512)
        o_sb = _sb_sample(q_sb, cache_sb_k[l].reshape(dec_batch, n_past, sbw),
                          cache_sb_v[l].reshape(dec_batch, n_past, sbw), k_sb, v_sb, g_sb, tk=256, head_dim=sb_dim)
        return o_mla, o_sb

    pos_p = jnp.arange(seq, dtype=jnp.int32)
    pos_s = n_past + jnp.arange(dec_seq, dtype=jnp.int32)
    yp, rp = layer(x_prompt.reshape(batch * seq, d), pos_p, 512, attend_prompt)
    ys, rs = layer(x_sample.reshape(dec_batch * dec_seq, d), pos_s, 512, attend_sample)

    def rows(r, b, s):
        lat, kr, kn, vn = r
        return (lat.reshape(1, b, s, kv_lora), kr.reshape(1, b, s, rope_dim),
                kn.reshape(1, b, s, sb_heads, sb_dim), vn.reshape(1, b, s, sb_heads, sb_dim))

    return (yp.reshape(batch, seq, d), ys.reshape(dec_batch, dec_seq, d)) + rows(rp, batch, seq) + rows(rs, dec_batch, dec_seq)
```

```python
import functools

import jax
import jax.numpy as jnp
from jax import lax
from jax.experimental import pallas as pl
from jax.experimental.pallas import tpu as pltpu

EPS = 1e-6
CHUNK = 64
ROPE_THETA = 10000.0
LANES = 128
F32 = jnp.float32
BF16 = jnp.bfloat16
VMEM_LIMIT = 56 * 1024 * 1024


def _dot(a, b):
    return jnp.dot(a, b, preferred_element_type=F32)


def _dot_nt(a, b):
    return lax.dot_general(a, b, (((1,), (1,)), ((), ())), preferred_element_type=F32)


def _dot_tn(a, b):
    return lax.dot_general(a, b, (((0,), (0,)), ((), ())), preferred_element_type=F32)


def _rms(x, g):
    return x * lax.rsqrt(jnp.mean(x * x, axis=-1, keepdims=True) + EPS) * g


def _params(*sem):
    return pltpu.CompilerParams(dimension_semantics=sem, vmem_limit_bytes=VMEM_LIMIT)


def _ffn_kernel(x_ref, gpre_ref, wg_ref, wu_ref, wd_ref, gpost_ref, gfin_ref, o_ref, *, final, fc):
    x = x_ref[...]
    xn = _rms(x, gpre_ref[...]).astype(BF16)
    d_ff = wg_ref.shape[1]
    acc = jnp.zeros(x.shape, F32)
    for c in range(0, d_ff, fc):
        g = _dot(xn, wg_ref[:, c:c + fc])
        u = _dot(xn, wu_ref[:, c:c + fc])
        a = (g * jax.nn.sigmoid(g)) * u
        acc = acc + _dot(a.astype(BF16), wd_ref[c:c + fc, :])
    h = x + 0.5 * _rms(acc, gpost_ref[...])
    if final:
        h = _rms(h, gfin_ref[...])
    o_ref[...] = h


def _ffn(x, g_pre, wg, wu, wd, g_post, g_fin, *, final, tm):
    t, d = x.shape
    d_ff = wg.shape[1]
    fc = d_ff // 2 if (d_ff // 2) % LANES == 0 else d_ff
    const = lambda i: (0, 0)
    return pl.pallas_call(
        functools.partial(_ffn_kernel, final=final, fc=fc),
        grid=(t // tm,),
        in_specs=[
            pl.BlockSpec((tm, d), lambda i: (i, 0)),
            pl.BlockSpec((1, d), const),
            pl.BlockSpec((d, d_ff), const, pipeline_mode=pl.Buffered(1)),
            pl.BlockSpec((d, d_ff), const, pipeline_mode=pl.Buffered(1)),
            pl.BlockSpec((d_ff, d), const, pipeline_mode=pl.Buffered(1)),
            pl.BlockSpec((1, d), const),
            pl.BlockSpec((1, d), const),
        ],
        out_specs=pl.BlockSpec((tm, d), lambda i: (i, 0)),
        out_shape=jax.ShapeDtypeStruct((t, d), F32),
        compiler_params=_params("parallel"),
        name="ffn_final" if final else "ffn",
    )(x, g_pre, wg, wu, wd, g_post, g_fin)


def _mix_in_kernel(h_ref, gpre_ref, win_ref, gq_ref, wuq_ref, wuk_ref, gkv_ref, cos_ref, sin_ref,
                   lat_ref, kr_ref, kn_ref, vn_ref, qcat_ref, kcat_ref, qsb_ref, ksb_ref, vsb_ref,
                   *, heads, q_lora, kv_lora, rope_dim, sbw, mla_scale, sb_scale):
    u = _rms(h_ref[...], gpre_ref[...]).astype(BF16)
    proj = _dot(u, win_ref[...])
    cosp = cos_ref[...]
    sinp = sin_ref[...]
    o1 = q_lora
    o2 = o1 + kv_lora
    o3 = o2 + LANES

    def rope(grp):
        return grp * cosp + pltpu.roll(grp, LANES - rope_dim, 1) * sinp

    latent = _rms(proj[:, o1:o2], gkv_ref[...])
    krope = rope(proj[:, o2:o3])
    lat_ref[...] = latent
    kr_ref[...] = krope[:, :rope_dim]
    k_sb = proj[:, o3 + sbw:o3 + 2 * sbw]
    v_sb = proj[:, o3 + 2 * sbw:o3 + 3 * sbw]
    kn_ref[...] = k_sb
    vn_ref[...] = v_sb
    ksb_ref[...] = k_sb.astype(BF16)
    vsb_ref[...] = v_sb.astype(BF16)
    qsb_ref[...] = (proj[:, o3:o3 + sbw] * sb_scale).astype(BF16)
    kcat_ref[...] = jnp.concatenate([latent, krope], axis=1).astype(BF16)

    c_q = _rms(proj[:, :o1], gq_ref[...]).astype(BF16)
    q = _dot(c_q, wuq_ref[...])
    n_nope = wuk_ref.shape[0]
    q_lat = _dot(q[:, :n_nope].astype(BF16), wuk_ref[...])
    for hd in range(heads):
        ql = q_lat[:, hd * kv_lora:(hd + 1) * kv_lora] * mla_scale
        qr = rope(q[:, n_nope + hd * LANES:n_nope + (hd + 1) * LANES]) * mla_scale
        qcat_ref[hd] = jnp.concatenate([ql, qr], axis=1).astype(BF16)


def _mix_in(h, g_pre, w_in_ext, g_q, w_uq_ext, w_uk_bd, g_kv, cosp, sinp, *, tm, dims):
    t, d = h.shape
    heads, q_lora, kv_lora, rope_dim, sbw, mla_scale, sb_scale = dims
    n_tab = cosp.shape[0] // tm
    const = lambda i: (0, 0)
    row = lambda i: (i, 0)
    tab = lambda i: (i % n_tab, 0)
    out_shape = (
        jax.ShapeDtypeStruct((t, kv_lora), F32),
        jax.ShapeDtypeStruct((t, rope_dim), F32),
        jax.ShapeDtypeStruct((t, sbw), F32),
        jax.ShapeDtypeStruct((t, sbw), F32),
        jax.ShapeDtypeStruct((heads, t, 2 * LANES), BF16),
        jax.ShapeDtypeStruct((t, 2 * LANES), BF16),
        jax.ShapeDtypeStruct((t, sbw), BF16),
        jax.ShapeDtypeStruct((t, sbw), BF16),
        jax.ShapeDtypeStruct((t, sbw), BF16),
    )
    out_specs = (
        pl.BlockSpec((tm, kv_lora), row),
        pl.BlockSpec((tm, rope_dim), row),
        pl.BlockSpec((tm, sbw), row),
        pl.BlockSpec((tm, sbw), row),
        pl.BlockSpec((heads, tm, 2 * LANES), lambda i: (0, i, 0)),
        pl.BlockSpec((tm, 2 * LANES), row),
        pl.BlockSpec((tm, sbw), row),
        pl.BlockSpec((tm, sbw), row),
        pl.BlockSpec((tm, sbw), row),
    )
    return pl.pallas_call(
        functools.partial(_mix_in_kernel, heads=heads, q_lora=q_lora, kv_lora=kv_lora,
                          rope_dim=rope_dim, sbw=sbw, mla_scale=mla_scale, sb_scale=sb_scale),
        grid=(t // tm,),
        in_specs=[
            pl.BlockSpec((tm, d), row),
            pl.BlockSpec((1, d), const),
            pl.BlockSpec(w_in_ext.shape, const),
            pl.BlockSpec((1, q_lora), const),
            pl.BlockSpec(w_uq_ext.shape, const),
            pl.BlockSpec(w_uk_bd.shape, const),
            pl.BlockSpec((1, kv_lora), const),
            pl.BlockSpec((tm, LANES), tab),
            pl.BlockSpec((tm, LANES), tab),
        ],
        out_specs=out_specs,
        out_shape=out_shape,
        compiler_params=_params("parallel"),
        name="mix_in",
    )(h, g_pre, w_in_ext, g_q, w_uq_ext, w_uk_bd, g_kv, cosp, sinp)


def _mla_update(qs, ks, vblk, m_ref, l_ref, acc_ref, visible):
    s = sum(_dot_nt(kk, qq) for qq, kk in zip(qs, ks))
    if visible is not None:
        s = jnp.where(visible, s, -jnp.inf)
    m_prev = m_ref[...]
    m_new = jnp.maximum(m_prev, jnp.max(s, axis=0, keepdims=True))
    alpha = jnp.exp(m_prev - m_new)
    p = jnp.exp(s - m_new)
    l_ref[...] = alpha * l_ref[...] + jnp.sum(p, axis=0, keepdims=True)
    acc_ref[...] = alpha * acc_ref[...] + _dot_tn(vblk, p.astype(BF16))
    m_ref[...] = m_new


def _mla_finish(l_ref, acc_ref, wuvt_ref, g_ref, o_ref, heads, tq):
    o_lat = (acc_ref[...] / l_ref[...]).astype(BF16)
    outs = []
    for hd in range(heads):
        o = _dot(wuvt_ref[hd], o_lat[:, hd * tq:(hd + 1) * tq])
        outs.append(o * lax.rsqrt(jnp.mean(o * o, axis=0, keepdims=True) + EPS) * g_ref[hd])
    o_t = jnp.concatenate(outs, axis=0).T
    o_ref[...] = o_t[:o_ref.shape[0]].astype(BF16)


def _mla_init(m_ref, l_ref, acc_ref):
    m_ref[...] = jnp.full(m_ref.shape, -jnp.inf, F32)
    l_ref[...] = jnp.zeros(l_ref.shape, F32)
    acc_ref[...] = jnp.zeros(acc_ref.shape, F32)


def _mla_prompt_kernel(q_ref, k_ref, wuvt_ref, g_ref, o_ref, m_ref, l_ref, acc_ref, *, heads, tq, tk, c_dim):
    j = pl.program_id(1)
    q0 = j * tq
    q = q_ref[...].reshape(heads * tq, q_ref.shape[2])
    _mla_init(m_ref, l_ref, acc_ref)
    n_full = q0 // tk

    def full_block(kb, carry):
        kblk = k_ref[pl.ds(pl.multiple_of(kb * tk, tk), tk), :]
        _mla_update([q], [kblk], kblk[:, :c_dim], m_ref, l_ref, acc_ref, None)
        return carry

    lax.fori_loop(0, n_full, full_block, 0)
    k0 = pl.multiple_of(n_full * tk, tk)
    kblk = k_ref[pl.ds(k0, tk), :]
    kpos = k0 + lax.broadcasted_iota(jnp.int32, (tk, heads * tq), 0)
    qpos = q0 + lax.broadcasted_iota(jnp.int32, (tk, heads * tq), 1) % tq
    visible = (kpos // CHUNK) <= (qpos // CHUNK)
    _mla_update([q], [kblk], kblk[:, :c_dim], m_ref, l_ref, acc_ref, visible)
    _mla_finish(l_ref, acc_ref, wuvt_ref, g_ref, o_ref, heads, tq)


def _mla_scratch(rows, c_dim):
    return [pltpu.VMEM((1, rows), F32), pltpu.VMEM((1, rows), F32), pltpu.VMEM((c_dim, rows), F32)]


def _mla_prompt(qcat, kcat, w_uv_t, g_out, *, batch, seq, tq, tk):
    heads, t, dk = qcat.shape
    v_dim, c_dim = w_uv_t.shape[1], w_uv_t.shape[2]
    assert tk % tq == 0 and tq % CHUNK == 0 and seq % tk == 0 and tq % LANES == 0
    nq = seq // tq
    return pl.pallas_call(
        functools.partial(_mla_prompt_kernel, heads=heads, tq=tq, tk=tk, c_dim=c_dim),
        grid=(batch, nq),
        in_specs=[
            pl.BlockSpec((heads, tq, dk), lambda b, j: (0, b * nq + j, 0)),
            pl.BlockSpec((seq, dk), lambda b, j: (b, 0)),
            pl.BlockSpec(w_uv_t.shape, lambda b, j: (0, 0, 0)),
            pl.BlockSpec(g_out.shape, lambda b, j: (0, 0, 0)),
        ],
        out_specs=pl.BlockSpec((tq, heads * v_dim), lambda b, j: (b * nq + j, 0)),
        out_shape=jax.ShapeDtypeStruct((t, heads * v_dim), BF16),
        scratch_shapes=_mla_scratch(heads * tq, c_dim),
        compiler_params=_params("parallel", "arbitrary"),
        name="mla_prompt",
    )(qcat, kcat, w_uv_t, g_out)


def _mla_sample_kernel(q_ref, lat_ref, kr_ref, knew_ref, wuvt_ref, g_ref, o_ref, m_ref, l_ref, acc_ref,
                       *, heads, tq, tk, c_dim, rope_dim):
    qh, dk = q_ref.shape[1], q_ref.shape[2]
    q = jnp.concatenate([q_ref[...], jnp.zeros((heads, tq - qh, dk), BF16)], axis=1).reshape(heads * tq, dk)
    q_lat = q[:, :c_dim]
    q_rope = q[:, c_dim:c_dim + rope_dim]
    _mla_init(m_ref, l_ref, acc_ref)
    n_past = lat_ref.shape[0]

    def past_block(kb, carry):
        k0 = pl.multiple_of(kb * tk, tk)
        lat = lat_ref[pl.ds(k0, tk), :].astype(BF16)
        kr = kr_ref[pl.ds(k0, tk), :].astype(BF16)
        _mla_update([q_lat, q_rope], [lat, kr], lat, m_ref, l_ref, acc_ref, None)
        return carry

    lax.fori_loop(0, n_past // tk, past_block, 0)
    knew = knew_ref[...]
    _mla_update([q], [knew], knew[:, :c_dim], m_ref, l_ref, acc_ref, None)
    _mla_finish(l_ref, acc_ref, wuvt_ref, g_ref, o_ref, heads, tq)


def _mla_sample(qcat, cache_lat, cache_kr, kcat, w_uv_t, g_out, *, tk):
    heads, t, dk = qcat.shape
    batch, n_past, c_dim = cache_lat.shape
    rope_dim = cache_kr.shape[2]
    qh = t // batch
    tq = -(-qh // LANES) * LANES
    v_dim = w_uv_t.shape[1]
    assert n_past % CHUNK == 0 and qh <= CHUNK and n_past % tk == 0
    return pl.pallas_call(
        functools.partial(_mla_sample_kernel, heads=heads, tq=tq, tk=tk, c_dim=c_dim, rope_dim=rope_dim),
        grid=(batch,),
        in_specs=[
            pl.BlockSpec((heads, qh, dk), lambda b: (0, b, 0)),
            pl.BlockSpec((None, n_past, c_dim), lambda b: (b, 0, 0)),
            pl.BlockSpec((None, n_past, rope_dim), lambda b: (b, 0, 0)),
            pl.BlockSpec((qh, dk), lambda b: (b, 0)),
            pl.BlockSpec(w_uv_t.shape, lambda b: (0, 0, 0)),
            pl.BlockSpec(g_out.shape, lambda b: (0, 0, 0)),
        ],
        out_specs=pl.BlockSpec((qh, heads * v_dim), lambda b: (b, 0)),
        out_shape=jax.ShapeDtypeStruct((t, heads * v_dim), BF16),
        scratch_shapes=_mla_scratch(heads * tq, c_dim),
        compiler_params=_params("parallel"),
        name="mla_sample",
    )(qcat, cache_lat, cache_kr, kcat, w_uv_t, g_out)


def _sb_update(q2, kblk, vblk, triu, carry, acc, before):
    z = _dot_nt(kblk, q2)
    sp = jnp.maximum(z, 0.0) + jnp.log(1.0 + jnp.exp(-jnp.abs(z)))
    if before is not None:
        sp = jnp.where(before, sp, 0.0)
    hi = sp.astype(BF16)
    lo = (sp - hi.astype(F32)).astype(BF16)
    c = _dot(triu, hi) + _dot(triu, lo) + carry
    a = jnp.exp(z - c)
    if before is not None:
        a = jnp.where(before, a, 0.0)
    acc = acc + _dot_tn(vblk, a.astype(BF16))
    return c[0:1, :], acc


def _sb_queries(q, head_dim):
    lane = lax.broadcasted_iota(jnp.int32, (1, q.shape[1]), 1) // head_dim
    return jnp.concatenate([jnp.where(lane == h, q, jnp.zeros_like(q)) for h in range(q.shape[1] // head_dim)], axis=0)


def _sb_before(tk, tq, n):
    key = lax.broadcasted_iota(jnp.int32, (tk, n), 0)
    qry = lax.broadcasted_iota(jnp.int32, (tk, n), 1) % tq
    return key < qry


def _sb_finish(acc, g_ref, o_ref, head_dim, tq):
    r = lax.broadcasted_iota(jnp.int32, acc.shape, 0) // head_dim
    c = lax.broadcasted_iota(jnp.int32, acc.shape, 1) // tq
    o = jnp.where(r == c, acc, 0.0)
    inv = lax.rsqrt(jnp.sum(o * o, axis=0, keepdims=True) / head_dim + EPS)
    o_t = (o * inv * g_ref[...]).T
    out = o_t[:tq]
    for h in range(1, acc.shape[1] // tq):
        out = out + o_t[h * tq:(h + 1) * tq]
    o_ref[...] = out.astype(BF16)


def _sb_prompt_kernel(q_ref, k_ref, v_ref, tri_ref, g_ref, o_ref, *, tq, head_dim):
    j = pl.program_id(2)
    q2 = _sb_queries(q_ref[...], head_dim)
    width = q_ref.shape[1]
    n = q2.shape[0]
    triu = tri_ref[...]
    k0 = pl.multiple_of(j * tq, tq)
    state = _sb_update(q2, k_ref[pl.ds(k0, tq), :], v_ref[pl.ds(k0, tq), :], triu,
                       jnp.zeros((1, n), F32), jnp.zeros((width, n), F32), _sb_before(tq, tq, n))

    def earlier_block(i, state):
        k0 = pl.multiple_of((j - 1 - i) * tq, tq)
        return _sb_update(q2, k_ref[pl.ds(k0, tq), :], v_ref[pl.ds(k0, tq), :], triu, state[0], state[1], None)

    state = lax.fori_loop(0, j, earlier_block, state)
    _sb_finish(state[1], g_ref, o_ref, head_dim, tq)


def _triu(n):
    r = lax.broadcasted_iota(jnp.int32, (n, n), 0)
    c = lax.broadcasted_iota(jnp.int32, (n, n), 1)
    return (c >= r).astype(BF16)


def _sb_prompt(q_sb, k_sb, v_sb, g_col, *, batch, seq, tq, head_dim):
    t, sbw = q_sb.shape
    nq = seq // tq
    n_grp = sbw // LANES
    return pl.pallas_call(
        functools.partial(_sb_prompt_kernel, tq=tq, head_dim=head_dim),
        grid=(batch, n_grp, nq),
        in_specs=[
            pl.BlockSpec((tq, LANES), lambda b, p, j: (b * nq + j, p)),
            pl.BlockSpec((seq, LANES), lambda b, p, j: (b, p)),
            pl.BlockSpec((seq, LANES), lambda b, p, j: (b, p)),
            pl.BlockSpec((tq, tq), lambda b, p, j: (0, 0)),
            pl.BlockSpec((LANES, 1), lambda b, p, j: (p, 0)),
        ],
        out_specs=pl.BlockSpec((tq, LANES), lambda b, p, j: (b * nq + j, p)),
        out_shape=jax.ShapeDtypeStruct((t, sbw), BF16),
        compiler_params=_params("parallel", "parallel", "arbitrary"),
        name="sb_prompt",
    )(q_sb, k_sb, v_sb, _triu(tq), g_col)


def _sb_sample_kernel(q_ref, kc_ref, vc_ref, kn_ref, vn_ref, tri_ref, g_ref, o_ref, *, tk, head_dim):
    tq, width = q_ref.shape
    q2 = _sb_queries(q_ref[...], head_dim)
    n = q2.shape[0]
    r = lax.broadcasted_iota(jnp.int32, (tq, tq), 0)
    c = lax.broadcasted_iota(jnp.int32, (tq, tq), 1)
    state = _sb_update(q2, kn_ref[...], vn_ref[...], (c >= r).astype(BF16),
                       jnp.zeros((1, n), F32), jnp.zeros((width, n), F32), _sb_before(tq, tq, n))
    triu = tri_ref[...]
    n_blk = kc_ref.shape[0] // tk

    def earlier_block(i, state):
        k0 = pl.multiple_of((n_blk - 1 - i) * tk, tk)
        return _sb_update(q2, kc_ref[pl.ds(k0, tk), :].astype(BF16), vc_ref[pl.ds(k0, tk), :].astype(BF16),
                          triu, state[0], state[1], None)

    state = lax.fori_loop(0, n_blk, earlier_block, state)
    _sb_finish(state[1], g_ref, o_ref, head_dim, tq)


def _sb_sample(q_sb, cache_k, cache_v, k_sb, v_sb, g_col, *, tk, head_dim):
    t, sbw = q_sb.shape
    batch, n_past, _ = cache_k.shape
    tq = t // batch
    n_grp = sbw // LANES
    assert n_past % tk == 0
    new = pl.BlockSpec((tq, LANES), lambda b, p: (b, p))
    past = pl.BlockSpec((None, n_past, LANES), lambda b, p: (b, 0, p))
    return pl.pallas_call(
        functools.partial(_sb_sample_kernel, tk=tk, head_dim=head_dim),
        grid=(batch, n_grp),
        in_specs=[new, past, past, new, new,
                  pl.BlockSpec((tk, tk), lambda b, p: (0, 0)),
                  pl.BlockSpec((LANES, 1), lambda b, p: (p, 0))],
        out_specs=new,
        out_shape=jax.ShapeDtypeStruct((t, sbw), BF16),
        compiler_params=_params("parallel", "parallel"),
        name="sb_sample",
    )(q_sb, cache_k, cache_v, k_sb, v_sb, _triu(tk), g_col)


def _out_proj_kernel(oa_ref, ob_ref, h_ref, wa_ref, wb_ref, g_ref, o_ref):
    m = _dot(oa_ref[...], wa_ref[...]) + _dot(ob_ref[...], wb_ref[...])
    o_ref[...] = h_ref[...] + _rms(m, g_ref[...])


def _out_proj(o_a, o_b, h, w_a, w_b, g_post, *, tm):
    t, d = h.shape
    row = lambda i: (i, 0)
    const = lambda i: (0, 0)
    return pl.pallas_call(
        _out_proj_kernel,
        grid=(t // tm,),
        in_specs=[
            pl.BlockSpec((tm, o_a.shape[1]), row),
            pl.BlockSpec((tm, o_b.shape[1]), row),
            pl.BlockSpec((tm, d), row),
            pl.BlockSpec(w_a.shape, const),
            pl.BlockSpec(w_b.shape, const),
            pl.BlockSpec((1, d), const),
        ],
        out_specs=pl.BlockSpec((tm, d), row),
        out_shape=jax.ShapeDtypeStruct((t, d), F32),
        compiler_params=_params("parallel"),
        name="out_proj",
    )(o_a, o_b, h, w_a, w_b, g_post)


def _rot_cols(w):
    half = w.shape[-1] // 2
    return jnp.concatenate([-w[..., half:], w[..., :half]], axis=-1)


def _rope_tables(pos, rope_dim, rows):
    half = rope_dim // 2
    inv_freq = ROPE_THETA ** (-jnp.arange(half, dtype=F32) / half)
    ang = pos.astype(F32)[:, None] * inv_freq[None, :]
    pad = jnp.zeros((pos.shape[0], LANES - rope_dim), F32)
    cosp = jnp.concatenate([jnp.cos(ang), jnp.cos(ang), pad], axis=1)
    sinp = jnp.concatenate([jnp.sin(ang), jnp.sin(ang), pad], axis=1)
    reps = max(1, rows // pos.shape[0])
    return jnp.tile(cosp, (reps, 1)), jnp.tile(sinp, (reps, 1))


def kernel(x_prompt, x_sample, cache_mla_latent, cache_mla_krope, cache_sb_k, cache_sb_v, g_pre_ff1, w_gate1, w_up1, w_down1, g_post_ff1, g_pre_mix, w_in, g_q, w_uq, g_kv, w_uk, w_uv, g_mla_out, g_sb_out, w_out, g_post_mix, g_pre_ff2, w_gate2, w_up2, w_down2, g_post_ff2, g_final):
    depth = w_in.shape[0]
    assert depth == 1
    batch, seq, d = x_prompt.shape
    dec_batch, dec_seq, _ = x_sample.shape
    n_past = cache_mla_latent.shape[2]
    _, q_lora, heads, qk_dim = w_uq.shape
    kv_lora, _, nope = w_uk.shape[1:]
    rope_dim = qk_dim - nope
    v_dim = w_uv.shape[3]
    sb_heads, sb_dim = g_sb_out.shape[1:]
    sbw = sb_heads * sb_dim
    mla_scale = float(qk_dim) ** -0.5
    sb_scale = float(sb_dim) ** -0.5
    assert kv_lora == LANES and 2 * rope_dim <= LANES and LANES % sb_dim == 0
    dims = (heads, q_lora, kv_lora, rope_dim, sbw, mla_scale, sb_scale)

    row = lambda g: g.reshape(1, -1).astype(F32)
    l = 0
    wi = w_in[l]
    i1 = q_lora + kv_lora
    i2 = i1 + rope_dim
    w_kr = wi[:, i1:i2]
    w_in_ext = jnp.concatenate(
        [wi[:, :i1], w_kr, _rot_cols(w_kr), jnp.zeros((d, LANES - 2 * rope_dim), F32), wi[:, i2:]], axis=1).astype(BF16)
    wq = w_uq[l]
    wq_rope = wq[:, :, nope:]
    wq_grp = jnp.concatenate(
        [wq_rope, _rot_cols(wq_rope), jnp.zeros((q_lora, heads, LANES - 2 * rope_dim), F32)], axis=2)
    w_uq_ext = jnp.concatenate(
        [wq[:, :, :nope].reshape(q_lora, heads * nope), wq_grp.reshape(q_lora, heads * LANES)], axis=1).astype(BF16)
    wk = jnp.transpose(w_uk[l], (1, 2, 0))
    eye = jnp.eye(heads, dtype=F32)
    w_uk_bd = (wk[:, :, None, :] * eye[:, None, :, None]).reshape(heads * nope, heads * kv_lora).astype(BF16)
    w_uv_t = jnp.transpose(w_uv[l], (1, 2, 0)).astype(BF16)
    w_out_a = w_out[l][:heads * v_dim].astype(BF16)
    w_out_b = w_out[l][heads * v_dim:].astype(BF16)
    ffn1 = (row(g_pre_ff1[l]), w_gate1[l].astype(BF16), w_up1[l].astype(BF16), w_down1[l].astype(BF16),
            row(g_post_ff1[l]), row(g_final[l]))
    ffn2 = (row(g_pre_ff2[l]), w_gate2[l].astype(BF16), w_up2[l].astype(BF16), w_down2[l].astype(BF16),
            row(g_post_ff2[l]), row(g_final[l]))
    g_mla = g_mla_out[l].reshape(heads, v_dim, 1).astype(F32)
    g_sb = g_sb_out[l].reshape(sbw, 1).astype(F32)

    def layer(x, pos, tm, attend):
        h = _ffn(x, *ffn1, final=False, tm=tm)
        cosp, sinp = _rope_tables(pos, rope_dim, tm)
        lat, kr, kn, vn, qcat, kcat, q_sb, k_sb, v_sb = _mix_in(
            h, row(g_pre_mix[l]), w_in_ext, row(g_q[l]), w_uq_ext, w_uk_bd, row(g_kv[l]), cosp, sinp,
            tm=tm, dims=dims)
        o_mla, o_sb = attend(qcat, kcat, q_sb, k_sb, v_sb)
        h = _out_proj(o_mla, o_sb, h, w_out_a, w_out_b, row(g_post_mix[l]), tm=tm)
        y = _ffn(h, *ffn2, final=True, tm=tm)
        return y, (lat, kr, kn, vn)

    def attend_prompt(qcat, kcat, q_sb, k_sb, v_sb):
        o_mla = _mla_prompt(qcat, kcat, w_uv_t, g_mla, batch=batch, seq=seq, tq=128, tk=256)
        o_sb = _sb_prompt(q_sb, k_sb, v_sb, g_sb, batch=batch, seq=seq, tq=256, head_dim=sb_dim)
        return o_mla, o_sb

    def attend_sample(qcat, kcat, q_sb, k_sb, v_sb):
        o_mla = _mla_sample(qcat, cache_mla_latent[l], cache_mla_krope[l], kcat, w_uv_t, g_mla, tk=512)
        o_sb = _sb_sample(q_sb, cache_sb_k[l].reshape(dec_batch, n_past, sbw),
                          cache_sb_v[l].reshape(dec_batch, n_past, sbw), k_sb, v_sb, g_sb, tk=256, head_dim=sb_dim)
        return o_mla, o_sb

    pos_p = jnp.arange(seq, dtype=jnp.int32)
    pos_s = n_past + jnp.arange(dec_seq, dtype=jnp.int32)
    yp, rp = layer(x_prompt.reshape(batch * seq, d), pos_p, 512, attend_prompt)
    ys, rs = layer(x_sample.reshape(dec_batch * dec_seq, d), pos_s, 512, attend_sample)

    def rows(r, b, s):
        lat, kr, kn, vn = r
        return (lat.reshape(1, b, s, kv_lora), kr.reshape(1, b, s, rope_dim),
                kn.reshape(1, b, s, sb_heads, sb_dim), vn.reshape(1, b, s, sb_heads, sb_dim))

    return (yp.reshape(batch, seq, d), ys.reshape(dec_batch, dec_seq, d)) + rows(rp, batch, seq) + rows(rs, dec_batch, dec_seq)
```

```python
import functools

import jax
import jax.numpy as jnp
from jax import lax
from jax.experimental import pallas as pl
from jax.experimental.pallas import tpu as pltpu

EPS = 1e-6
CHUNK = 64
ROPE_THETA = 10000.0
LANES = 128
F32 = jnp.float32
BF16 = jnp.bfloat16
VMEM_LIMIT = 56 * 1024 * 1024
SUBLANES = 8
ONES_ROWS = 2 * SUBLANES
LOG2E = 1.4426950408889634
DEAD_EXPONENT = -120.0
NORM_MARGIN = 1.02


def _dot(a, b):
    return jnp.dot(a, b, preferred_element_type=F32)


def _dot_nt(a, b):
    return lax.dot_general(a, b, (((1,), (1,)), ((), ())), preferred_element_type=F32)


def _dot_tn(a, b):
    return lax.dot_general(a, b, (((0,), (0,)), ((), ())), preferred_element_type=F32)


def _rms(x, g):
    return x * lax.rsqrt(jnp.mean(x * x, axis=-1, keepdims=True) + EPS) * g


def _params(*sem):
    return pltpu.CompilerParams(dimension_semantics=sem, vmem_limit_bytes=VMEM_LIMIT)


def _ffn_kernel(x_ref, gpre_ref, wg_ref, wu_ref, wd_ref, gpost_ref, gfin_ref, o_ref, *, final, fc):
    x = x_ref[...]
    xn = _rms(x, gpre_ref[...]).astype(BF16)
    d_ff = wg_ref.shape[1]
    acc = jnp.zeros(x.shape, F32)
    for c in range(0, d_ff, fc):
        g = _dot(xn, wg_ref[:, c:c + fc])
        u = _dot(xn, wu_ref[:, c:c + fc])
        a = (g * jax.nn.sigmoid(g)) * u
        acc = acc + _dot(a.astype(BF16), wd_ref[c:c + fc, :])
    h = x + 0.5 * _rms(acc, gpost_ref[...])
    if final:
        h = _rms(h, gfin_ref[...])
    o_ref[...] = h


def _ffn(x, g_pre, wg, wu, wd, g_post, g_fin, *, final, tm):
    t, d = x.shape
    d_ff = wg.shape[1]
    fc = d_ff // 2 if (d_ff // 2) % LANES == 0 else d_ff
    const = lambda i: (0, 0)
    return pl.pallas_call(
        functools.partial(_ffn_kernel, final=final, fc=fc),
        grid=(t // tm,),
        in_specs=[
            pl.BlockSpec((tm, d), lambda i: (i, 0)),
            pl.BlockSpec((1, d), const),
            pl.BlockSpec((d, d_ff), const, pipeline_mode=pl.Buffered(1)),
            pl.BlockSpec((d, d_ff), const, pipeline_mode=pl.Buffered(1)),
            pl.BlockSpec((d_ff, d), const, pipeline_mode=pl.Buffered(1)),
            pl.BlockSpec((1, d), const),
            pl.BlockSpec((1, d), const),
        ],
        out_specs=pl.BlockSpec((tm, d), lambda i: (i, 0)),
        out_shape=jax.ShapeDtypeStruct((t, d), F32),
        compiler_params=_params("parallel"),
        name="ffn_final" if final else "ffn",
    )(x, g_pre, wg, wu, wd, g_post, g_fin)


def _mix_in_kernel(h_ref, gpre_ref, win_ref, gq_ref, wuq_ref, wuk_ref, gkv_ref, cos_ref, sin_ref, sel_ref,
                   lat_ref, kr_ref, kn_ref, vn_ref, qcat_ref, kcat_ref, latt_ref, qsb_ref, ksb_ref, vsb_ref, ksq_ref,
                   *, heads, q_lora, kv_lora, rope_dim, sbw, mla_scale, sb_scale):
    u = _rms(h_ref[...], gpre_ref[...]).astype(BF16)
    proj = _dot(u, win_ref[...])
    cosp = cos_ref[...]
    sinp = sin_ref[...]
    o1 = q_lora
    o2 = o1 + kv_lora
    o3 = o2 + LANES

    def rope(grp):
        return grp * cosp + pltpu.roll(grp, LANES - rope_dim, 1) * sinp

    latent = _rms(proj[:, o1:o2], gkv_ref[...])
    krope = rope(proj[:, o2:o3])
    lat_ref[...] = latent
    kr_ref[...] = krope[:, :rope_dim]
    k_sb = proj[:, o3 + sbw:o3 + 2 * sbw]
    v_sb = proj[:, o3 + 2 * sbw:o3 + 3 * sbw]
    kn_ref[...] = k_sb
    vn_ref[...] = v_sb
    k_bf = k_sb.astype(BF16)
    ksb_ref[...] = k_bf
    vsb_ref[...] = v_sb.astype(BF16)
    qsb_ref[...] = (proj[:, o3:o3 + sbw] * sb_scale).astype(BF16)
    kcat_ref[...] = jnp.concatenate([latent, krope], axis=1).astype(BF16)
    latt_ref[...] = jnp.concatenate([latent.T, jnp.ones((ONES_ROWS, latent.shape[0]), F32)], axis=0).astype(BF16)
    k_f = k_bf.astype(F32)
    ksq_ref[...] = _dot((k_f * k_f).astype(BF16), sel_ref[...])

    c_q = _rms(proj[:, :o1], gq_ref[...]).astype(BF16)
    q = _dot(c_q, wuq_ref[...])
    n_nope = wuk_ref.shape[0]
    q_lat = _dot(q[:, :n_nope].astype(BF16), wuk_ref[...])
    for hd in range(heads):
        ql = q_lat[:, hd * kv_lora:(hd + 1) * kv_lora] * mla_scale
        qr = rope(q[:, n_nope + hd * LANES:n_nope + (hd + 1) * LANES]) * mla_scale
        qcat_ref[hd] = jnp.concatenate([ql, qr], axis=1).astype(BF16)


def _mix_in(h, g_pre, w_in_ext, g_q, w_uq_ext, w_uk_bd, g_kv, cosp, sinp, *, tm, dims):
    t, d = h.shape
    heads, q_lora, kv_lora, rope_dim, sbw, sb_heads, mla_scale, sb_scale = dims
    n_tab = cosp.shape[0] // tm
    const = lambda i: (0, 0)
    row = lambda i: (i, 0)
    tab = lambda i: (i % n_tab, 0)
    out_shape = (
        jax.ShapeDtypeStruct((t, kv_lora), F32),
        jax.ShapeDtypeStruct((t, rope_dim), F32),
        jax.ShapeDtypeStruct((t, sbw), F32),
        jax.ShapeDtypeStruct((t, sbw), F32),
        jax.ShapeDtypeStruct((heads, t, 2 * LANES), BF16),
        jax.ShapeDtypeStruct((t, 2 * LANES), BF16),
        jax.ShapeDtypeStruct((kv_lora + ONES_ROWS, t), BF16),
        jax.ShapeDtypeStruct((t, sbw), BF16),
        jax.ShapeDtypeStruct((t, sbw), BF16),
        jax.ShapeDtypeStruct((t, sbw), BF16),
        jax.ShapeDtypeStruct((t, LANES), F32),
    )
    out_specs = (
        pl.BlockSpec((tm, kv_lora), row),
        pl.BlockSpec((tm, rope_dim), row),
        pl.BlockSpec((tm, sbw), row),
        pl.BlockSpec((tm, sbw), row),
        pl.BlockSpec((heads, tm, 2 * LANES), lambda i: (0, i, 0)),
        pl.BlockSpec((tm, 2 * LANES), row),
        pl.BlockSpec((kv_lora + ONES_ROWS, tm), lambda i: (0, i)),
        pl.BlockSpec((tm, sbw), row),
        pl.BlockSpec((tm, sbw), row),
        pl.BlockSpec((tm, sbw), row),
        pl.BlockSpec((tm, LANES), row),
    )
    sel = (lax.broadcasted_iota(jnp.int32, (sbw, LANES), 0) // (sbw // sb_heads)
           == lax.broadcasted_iota(jnp.int32, (sbw, LANES), 1)).astype(BF16)
    return pl.pallas_call(
        functools.partial(_mix_in_kernel, heads=heads, q_lora=q_lora, kv_lora=kv_lora,
                          rope_dim=rope_dim, sbw=sbw, mla_scale=mla_scale, sb_scale=sb_scale),
        grid=(t // tm,),
        in_specs=[
            pl.BlockSpec((tm, d), row),
            pl.BlockSpec((1, d), const),
            pl.BlockSpec(w_in_ext.shape, const),
            pl.BlockSpec((1, q_lora), const),
            pl.BlockSpec(w_uq_ext.shape, const),
            pl.BlockSpec(w_uk_bd.shape, const),
            pl.BlockSpec((1, kv_lora), const),
            pl.BlockSpec((tm, LANES), tab),
            pl.BlockSpec((tm, LANES), tab),
            pl.BlockSpec((sbw, LANES), const),
        ],
        out_specs=out_specs,
        out_shape=out_shape,
        compiler_params=_params("parallel"),
        name="mix_in",
    )(h, g_pre, w_in_ext, g_q, w_uq_ext, w_uk_bd, g_kv, cosp, sinp, sel)


def _mla_update(qs, ks, v, m_ref, acc_ref, visible, v_transposed):
    s = sum(_dot_nt(kk, qq) for qq, kk in zip(qs, ks))
    if visible is not None:
        s = jnp.where(visible, s, -jnp.inf)
    m_prev = m_ref[...]
    m_new = jnp.maximum(m_prev, jnp.max(s, axis=0, keepdims=True))
    alpha = jnp.exp2(m_prev - m_new)
    p = jnp.exp2(s - m_new)
    pb = p.astype(BF16)
    if v_transposed:
        pv = _dot(v, pb)
    else:
        l = jnp.sum(p, axis=0, keepdims=True)
        pv = jnp.concatenate([_dot_tn(v, pb), jnp.broadcast_to(l, (ONES_ROWS, l.shape[1]))], axis=0)
    acc_ref[...] = alpha * acc_ref[...] + pv
    m_ref[...] = m_new


def _mla_finish(acc_ref, wuvt_ref, g_ref, o_ref, heads, tq):
    c_dim = acc_ref.shape[0] - ONES_ROWS
    o_lat = (acc_ref[:c_dim, :] / acc_ref[c_dim:c_dim + 1, :]).astype(BF16)
    outs = []
    for hd in range(heads):
        o = _dot(wuvt_ref[hd], o_lat[:, hd * tq:(hd + 1) * tq])
        outs.append(o * lax.rsqrt(jnp.mean(o * o, axis=0, keepdims=True) + EPS) * g_ref[hd])
    o_t = jnp.concatenate(outs, axis=0).T
    o_ref[...] = o_t[:o_ref.shape[0]].astype(BF16)


def _mla_init(m_ref, acc_ref):
    m_ref[...] = jnp.full(m_ref.shape, -jnp.inf, F32)
    acc_ref[...] = jnp.zeros(acc_ref.shape, F32)


def _mla_prompt_kernel(q_ref, k_ref, vt_ref, wuvt_ref, g_ref, o_ref, m_ref, acc_ref, *, heads, tq, tk):
    j = pl.program_id(1)
    q0 = j * tq
    q = q_ref[...].reshape(heads * tq, q_ref.shape[2])
    _mla_init(m_ref, acc_ref)
    n_full = q0 // tk

    def full_block(kb, carry):
        k0 = pl.multiple_of(kb * tk, tk)
        _mla_update([q], [k_ref[pl.ds(k0, tk), :]], vt_ref[:, pl.ds(k0, tk)], m_ref, acc_ref, None, True)
        return carry

    lax.fori_loop(0, n_full, full_block, 0)
    k0 = pl.multiple_of(n_full * tk, tk)
    kpos = k0 + lax.broadcasted_iota(jnp.int32, (tk, heads * tq), 0)
    qpos = q0 + lax.broadcasted_iota(jnp.int32, (tk, heads * tq), 1) % tq
    visible = (kpos // CHUNK) <= (qpos // CHUNK)
    _mla_update([q], [k_ref[pl.ds(k0, tk), :]], vt_ref[:, pl.ds(k0, tk)], m_ref, acc_ref, visible, True)
    _mla_finish(acc_ref, wuvt_ref, g_ref, o_ref, heads, tq)


def _mla_scratch(rows, c_dim):
    return [pltpu.VMEM((1, rows), F32), pltpu.VMEM((c_dim + ONES_ROWS, rows), F32)]


def _mla_prompt(qcat, kcat, lat_t, w_uv_t, g_out, *, batch, seq, tq, tk):
    heads, t, dk = qcat.shape
    v_dim, c_dim = w_uv_t.shape[1], w_uv_t.shape[2]
    assert tk % tq == 0 and tq % CHUNK == 0 and seq % tk == 0 and tq % LANES == 0
    nq = seq // tq
    return pl.pallas_call(
        functools.partial(_mla_prompt_kernel, heads=heads, tq=tq, tk=tk),
        grid=(batch, nq),
        in_specs=[
            pl.BlockSpec((heads, tq, dk), lambda b, j: (0, b * nq + j, 0)),
            pl.BlockSpec((seq, dk), lambda b, j: (b, 0)),
            pl.BlockSpec((lat_t.shape[0], seq), lambda b, j: (0, b)),
            pl.BlockSpec(w_uv_t.shape, lambda b, j: (0, 0, 0)),
            pl.BlockSpec(g_out.shape, lambda b, j: (0, 0, 0)),
        ],
        out_specs=pl.BlockSpec((tq, heads * v_dim), lambda b, j: (b * nq + j, 0)),
        out_shape=jax.ShapeDtypeStruct((t, heads * v_dim), BF16),
        scratch_shapes=_mla_scratch(heads * tq, c_dim),
        compiler_params=_params("parallel", "arbitrary"),
        name="mla_prompt",
    )(qcat, kcat, lat_t, w_uv_t, g_out)


def _mla_sample_kernel(q_ref, lat_ref, kr_ref, knew_ref, wuvt_ref, g_ref, o_ref, m_ref, acc_ref,
                       *, heads, tq, tk, c_dim, rope_dim):
    qh, dk = q_ref.shape[1], q_ref.shape[2]
    q = jnp.concatenate([q_ref[...], jnp.zeros((heads, tq - qh, dk), BF16)], axis=1).reshape(heads * tq, dk)
    q_lat = q[:, :c_dim]
    q_rope = q[:, c_dim:c_dim + rope_dim]
    _mla_init(m_ref, acc_ref)
    n_past = lat_ref.shape[0]

    def past_block(kb, carry):
        k0 = pl.multiple_of(kb * tk, tk)
        lat = lat_ref[pl.ds(k0, tk), :].astype(BF16)
        kr = kr_ref[pl.ds(k0, tk), :].astype(BF16)
        _mla_update([q_lat, q_rope], [lat, kr], lat, m_ref, acc_ref, None, False)
        return carry

    lax.fori_loop(0, n_past // tk, past_block, 0)
    knew = knew_ref[...]
    _mla_update([q], [knew], knew[:, :c_dim], m_ref, acc_ref, None, False)
    _mla_finish(acc_ref, wuvt_ref, g_ref, o_ref, heads, tq)


def _mla_sample(qcat, cache_lat, cache_kr, kcat, w_uv_t, g_out, *, tk):
    heads, t, dk = qcat.shape
    batch, n_past, c_dim = cache_lat.shape
    rope_dim = cache_kr.shape[2]
    qh = t // batch
    tq = -(-qh // LANES) * LANES
    v_dim = w_uv_t.shape[1]
    assert n_past % CHUNK == 0 and qh <= CHUNK and n_past % tk == 0
    return pl.pallas_call(
        functools.partial(_mla_sample_kernel, heads=heads, tq=tq, tk=tk, c_dim=c_dim, rope_dim=rope_dim),
        grid=(batch,),
        in_specs=[
            pl.BlockSpec((heads, qh, dk), lambda b: (0, b, 0)),
            pl.BlockSpec((None, n_past, c_dim), lambda b: (b, 0, 0)),
            pl.BlockSpec((None, n_past, rope_dim), lambda b: (b, 0, 0)),
            pl.BlockSpec((qh, dk), lambda b: (b, 0)),
            pl.BlockSpec(w_uv_t.shape, lambda b: (0, 0, 0)),
            pl.BlockSpec(g_out.shape, lambda b: (0, 0, 0)),
        ],
        out_specs=pl.BlockSpec((qh, heads * v_dim), lambda b: (b, 0)),
        out_shape=jax.ShapeDtypeStruct((t, heads * v_dim), BF16),
        scratch_shapes=_mla_scratch(heads * tq, c_dim),
        compiler_params=_params("parallel"),
        name="mla_sample",
    )(qcat, cache_lat, cache_kr, kcat, w_uv_t, g_out)


def _sb_update(q2, kblk, vblk, triu, carry, acc, before, skip_dead=False):
    z = _dot_nt(kblk, q2)

    def live():
        sp = jnp.maximum(z, 0.0) + jnp.log(1.0 + jnp.exp(-jnp.abs(z)))
        if before is not None:
            sp = jnp.where(before, sp, 0.0)
        hi = sp.astype(BF16)
        lo = (sp - hi.astype(F32)).astype(BF16)
        c = _dot(triu, hi) + _dot(triu, lo) + carry
        a = jnp.exp(z - c)
        if before is not None:
            a = jnp.where(before, a, 0.0)
        return c[0:1, :], acc + _dot_tn(vblk() if callable(vblk) else vblk, a.astype(BF16))

    if not skip_dead:
        return live()
    return lax.cond(jnp.max(z - carry) > DEAD_EXPONENT, live, lambda: (carry, acc))


def _sb_queries(q, head_dim):
    lane = lax.broadcasted_iota(jnp.int32, (1, q.shape[1]), 1) // head_dim
    return jnp.concatenate([jnp.where(lane == h, q, jnp.zeros_like(q)) for h in range(q.shape[1] // head_dim)], axis=0)


def _sb_before(tk, tq, n):
    key = lax.broadcasted_iota(jnp.int32, (tk, n), 0)
    qry = lax.broadcasted_iota(jnp.int32, (tk, n), 1) % tq
    return key < qry


def _sb_finish(acc, g_ref, o_ref, head_dim, tq):
    r = lax.broadcasted_iota(jnp.int32, acc.shape, 0) // head_dim
    c = lax.broadcasted_iota(jnp.int32, acc.shape, 1) // tq
    o = jnp.where(r == c, acc, 0.0)
    inv = lax.rsqrt(jnp.sum(o * o, axis=0, keepdims=True) / head_dim + EPS)
    o_t = (o * inv * g_ref[...]).T
    out = o_t[:tq]
    for h in range(1, acc.shape[1] // tq):
        out = out + o_t[h * tq:(h + 1) * tq]
    o_ref[...] = out.astype(BF16)


def _sb_prompt_kernel(kmax_ref, q_ref, k_ref, v_ref, tri_ref, g_ref, o_ref, *, tq, head_dim):
    b, p, j = pl.program_id(0), pl.program_id(1), pl.program_id(2)
    n_blk = pl.num_programs(2)
    q2 = _sb_queries(q_ref[...], head_dim)
    width = q_ref.shape[1]
    n = q2.shape[0]
    hpg = width // head_dim
    triu = tri_ref[...]
    k0 = pl.multiple_of(j * tq, tq)
    carry, acc = _sb_update(q2, k_ref[pl.ds(k0, tq), :], v_ref[pl.ds(k0, tq), :], triu,
                            jnp.zeros((1, n), F32), jnp.zeros((width, n), F32), _sb_before(tq, tq, n))
    q2f = q2.astype(F32)
    qn = jnp.sqrt(_dot_nt(jnp.ones((SUBLANES, width), BF16), (q2f * q2f).astype(BF16))[0:1, :]) * NORM_MARGIN
    lane_head = lax.broadcasted_iota(jnp.int32, (1, n), 1) // tq

    def alive(kb, carry):
        base = (b * n_blk + jnp.maximum(kb, 0)) * (hpg * pl.num_programs(1)) + p * hpg
        kmax = jnp.zeros((1, n), F32)
        for h in range(hpg):
            kmax = jnp.where(lane_head == h, kmax_ref[base + h], kmax)
        return jnp.logical_and(kb >= 0, jnp.max(qn * kmax - carry) > DEAD_EXPONENT)

    def cond(state):
        return state[1]

    def earlier_block(state):
        kb, _, carry, acc = state
        k0 = pl.multiple_of(kb * tq, tq)
        carry, acc = _sb_update(q2, k_ref[pl.ds(k0, tq), :], v_ref[pl.ds(k0, tq), :], triu, carry, acc, None)
        return kb - 1, alive(kb - 1, carry), carry, acc

    state = lax.while_loop(cond, earlier_block, (j - 1, alive(j - 1, carry), carry, acc))
    _sb_finish(state[3], g_ref, o_ref, head_dim, tq)


def _triu(n):
    r = lax.broadcasted_iota(jnp.int32, (n, n), 0)
    c = lax.broadcasted_iota(jnp.int32, (n, n), 1)
    return (c >= r).astype(BF16)


def _sb_prompt(q_sb, k_sb, v_sb, ksq, g_col, *, batch, seq, tq, head_dim):
    t, sbw = q_sb.shape
    nq = seq // tq
    n_grp = sbw // LANES
    n_heads = sbw // head_dim
    blk_max = jnp.max(ksq[:, :n_heads].reshape(batch, nq, tq, n_heads), axis=2)
    kmax = (jnp.sqrt(lax.cummax(blk_max, axis=1)) * NORM_MARGIN).reshape(-1)
    return pl.pallas_call(
        functools.partial(_sb_prompt_kernel, tq=tq, head_dim=head_dim),
        grid_spec=pltpu.PrefetchScalarGridSpec(
            num_scalar_prefetch=1,
            grid=(batch, n_grp, nq),
            in_specs=[
                pl.BlockSpec((tq, LANES), lambda b, p, j, kmax: (b * nq + j, p)),
                pl.BlockSpec((seq, LANES), lambda b, p, j, kmax: (b, p)),
                pl.BlockSpec((seq, LANES), lambda b, p, j, kmax: (b, p)),
                pl.BlockSpec((tq, tq), lambda b, p, j, kmax: (0, 0)),
                pl.BlockSpec((LANES, 1), lambda b, p, j, kmax: (p, 0)),
            ],
            out_specs=pl.BlockSpec((tq, LANES), lambda b, p, j, kmax: (b * nq + j, p)),
        ),
        out_shape=jax.ShapeDtypeStruct((t, sbw), BF16),
        compiler_params=_params("parallel", "parallel", "arbitrary"),
        name="sb_prompt",
    )(kmax, q_sb, k_sb, v_sb, _triu(tq), g_col)


def _sb_sample_kernel(q_ref, kc_ref, vc_ref, kn_ref, vn_ref, tri_ref, g_ref, o_ref, *, tk, head_dim):
    tq, width = q_ref.shape
    q2 = _sb_queries(q_ref[...], head_dim)
    n = q2.shape[0]
    r = lax.broadcasted_iota(jnp.int32, (tq, tq), 0)
    c = lax.broadcasted_iota(jnp.int32, (tq, tq), 1)
    state = _sb_update(q2, kn_ref[...], vn_ref[...], (c >= r).astype(BF16),
                       jnp.zeros((1, n), F32), jnp.zeros((width, n), F32), _sb_before(tq, tq, n))
    triu = tri_ref[...]
    n_blk = kc_ref.shape[0] // tk

    def earlier_block(i, state):
        k0 = pl.multiple_of((n_blk - 1 - i) * tk, tk)
        return _sb_update(q2, kc_ref[pl.ds(k0, tk), :].astype(BF16), lambda: vc_ref[pl.ds(k0, tk), :].astype(BF16),
                          triu, state[0], state[1], None, skip_dead=True)

    state = lax.fori_loop(0, n_blk, earlier_block, state)
    _sb_finish(state[1], g_ref, o_ref, head_dim, tq)


def _sb_sample(q_sb, cache_k, cache_v, k_sb, v_sb, g_col, *, tk, head_dim):
    t, sbw = q_sb.shape
    batch, n_past, _ = cache_k.shape
    tq = t // batch
    n_grp = sbw // LANES
    assert n_past % tk == 0
    new = pl.BlockSpec((tq, LANES), lambda b, p: (b, p))
    past = pl.BlockSpec((None, n_past, LANES), lambda b, p: (b, 0, p))
    return pl.pallas_call(
        functools.partial(_sb_sample_kernel, tk=tk, head_dim=head_dim),
        grid=(batch, n_grp),
        in_specs=[new, past, past, new, new,
                  pl.BlockSpec((tk, tk), lambda b, p: (0, 0)),
                  pl.BlockSpec((LANES, 1), lambda b, p: (p, 0))],
        out_specs=new,
        out_shape=jax.ShapeDtypeStruct((t, sbw), BF16),
        compiler_params=_params("parallel", "parallel"),
        name="sb_sample",
    )(q_sb, cache_k, cache_v, k_sb, v_sb, _triu(tk), g_col)


def _out_proj_kernel(oa_ref, ob_ref, h_ref, wa_ref, wb_ref, g_ref, o_ref):
    m = _dot(oa_ref[...], wa_ref[...]) + _dot(ob_ref[...], wb_ref[...])
    o_ref[...] = h_ref[...] + _rms(m, g_ref[...])


def _out_proj(o_a, o_b, h, w_a, w_b, g_post, *, tm):
    t, d = h.shape
    row = lambda i: (i, 0)
    const = lambda i: (0, 0)
    return pl.pallas_call(
        _out_proj_kernel,
        grid=(t // tm,),
        in_specs=[
            pl.BlockSpec((tm, o_a.shape[1]), row),
            pl.BlockSpec((tm, o_b.shape[1]), row),
            pl.BlockSpec((tm, d), row),
            pl.BlockSpec(w_a.shape, const),
            pl.BlockSpec(w_b.shape, const),
            pl.BlockSpec((1, d), const),
        ],
        out_specs=pl.BlockSpec((tm, d), row),
        out_shape=jax.ShapeDtypeStruct((t, d), F32),
        compiler_params=_params("parallel"),
        name="out_proj",
    )(o_a, o_b, h, w_a, w_b, g_post)


def _rot_cols(w):
    half = w.shape[-1] // 2
    return jnp.concatenate([-w[..., half:], w[..., :half]], axis=-1)


def _rope_tables(pos, rope_dim, rows):
    half = rope_dim // 2
    inv_freq = ROPE_THETA ** (-jnp.arange(half, dtype=F32) / half)
    ang = pos.astype(F32)[:, None] * inv_freq[None, :]
    pad = jnp.zeros((pos.shape[0], LANES - rope_dim), F32)
    cosp = jnp.concatenate([jnp.cos(ang), jnp.cos(ang), pad], axis=1)
    sinp = jnp.concatenate([jnp.sin(ang), jnp.sin(ang), pad], axis=1)
    reps = max(1, rows // pos.shape[0])
    return jnp.tile(cosp, (reps, 1)), jnp.tile(sinp, (reps, 1))


def kernel(x_prompt, x_sample, cache_mla_latent, cache_mla_krope, cache_sb_k, cache_sb_v, g_pre_ff1, w_gate1, w_up1, w_down1, g_post_ff1, g_pre_mix, w_in, g_q, w_uq, g_kv, w_uk, w_uv, g_mla_out, g_sb_out, w_out, g_post_mix, g_pre_ff2, w_gate2, w_up2, w_down2, g_post_ff2, g_final):
    depth = w_in.shape[0]
    assert depth == 1
    batch, seq, d = x_prompt.shape
    dec_batch, dec_seq, _ = x_sample.shape
    n_past = cache_mla_latent.shape[2]
    _, q_lora, heads, qk_dim = w_uq.shape
    kv_lora, _, nope = w_uk.shape[1:]
    rope_dim = qk_dim - nope
    v_dim = w_uv.shape[3]
    sb_heads, sb_dim = g_sb_out.shape[1:]
    sbw = sb_heads * sb_dim
    mla_scale = float(qk_dim) ** -0.5
    sb_scale = float(sb_dim) ** -0.5
    assert kv_lora == LANES and 2 * rope_dim <= LANES and LANES % sb_dim == 0
    dims = (heads, q_lora, kv_lora, rope_dim, sbw, sb_heads, mla_scale * LOG2E, sb_scale)

    row = lambda g: g.reshape(1, -1).astype(F32)
    l = 0
    wi = w_in[l]
    i1 = q_lora + kv_lora
    i2 = i1 + rope_dim
    w_kr = wi[:, i1:i2]
    w_in_ext = jnp.concatenate(
        [wi[:, :i1], w_kr, _rot_cols(w_kr), jnp.zeros((d, LANES - 2 * rope_dim), F32), wi[:, i2:]], axis=1).astype(BF16)
    wq = w_uq[l]
    wq_rope = wq[:, :, nope:]
    wq_grp = jnp.concatenate(
        [wq_rope, _rot_cols(wq_rope), jnp.zeros((q_lora, heads, LANES - 2 * rope_dim), F32)], axis=2)
    w_uq_ext = jnp.concatenate(
        [wq[:, :, :nope].reshape(q_lora, heads * nope), wq_grp.reshape(q_lora, heads * LANES)], axis=1).astype(BF16)
    wk = jnp.transpose(w_uk[l], (1, 2, 0))
    eye = jnp.eye(heads, dtype=F32)
    w_uk_bd = (wk[:, :, None, :] * eye[:, None, :, None]).reshape(heads * nope, heads * kv_lora).astype(BF16)
    w_uv_t = jnp.transpose(w_uv[l], (1, 2, 0)).astype(BF16)
    w_out_a = w_out[l][:heads * v_dim].astype(BF16)
    w_out_b = w_out[l][heads * v_dim:].astype(BF16)
    ffn1 = (row(g_pre_ff1[l]), w_gate1[l].astype(BF16), w_up1[l].astype(BF16), w_down1[l].astype(BF16),
            row(g_post_ff1[l]), row(g_final[l]))
    ffn2 = (row(g_pre_ff2[l]), w_gate2[l].astype(BF16), w_up2[l].astype(BF16), w_down2[l].astype(BF16),
            row(g_post_ff2[l]), row(g_final[l]))
    g_mla = g_mla_out[l].reshape(heads, v_dim, 1).astype(F32)
    g_sb = g_sb_out[l].reshape(sbw, 1).astype(F32)

    def layer(x, pos, tm, attend):
        h = _ffn(x, *ffn1, final=False, tm=tm)
        cosp, sinp = _rope_tables(pos, rope_dim, tm)
        lat, kr, kn, vn, qcat, kcat, lat_t, q_sb, k_sb, v_sb, ksq = _mix_in(
            h, row(g_pre_mix[l]), w_in_ext, row(g_q[l]), w_uq_ext, w_uk_bd, row(g_kv[l]), cosp, sinp,
            tm=tm, dims=dims)
        o_mla, o_sb = attend(qcat, kcat, lat_t, q_sb, k_sb, v_sb, ksq)
        h = _out_proj(o_mla, o_sb, h, w_out_a, w_out_b, row(g_post_mix[l]), tm=tm)
        y = _ffn(h, *ffn2, final=True, tm=tm)
        return y, (lat, kr, kn, vn)

    def attend_prompt(qcat, kcat, lat_t, q_sb, k_sb, v_sb, ksq):
        o_mla = _mla_prompt(qcat, kcat, lat_t, w_uv_t, g_mla, batch=batch, seq=seq, tq=256, tk=512)
        o_sb = _sb_prompt(q_sb, k_sb, v_sb, ksq, g_sb, batch=batch, seq=seq, tq=256, head_dim=sb_dim)
        return o_mla, o_sb

    def attend_sample(qcat, kcat, lat_t, q_sb, k_sb, v_sb, ksq):
        o_mla = _mla_sample(qcat, cache_mla_latent[l], cache_mla_krope[l], kcat, w_uv_t, g_mla, tk=512)
        o_sb = _sb_sample(q_sb, cache_sb_k[l].reshape(dec_batch, n_past, sbw),
                          cache_sb_v[l].reshape(dec_batch, n_past, sbw), k_sb, v_sb, g_sb, tk=256, head_dim=sb_dim)
        return o_mla, o_sb

    pos_p = jnp.arange(seq, dtype=jnp.int32)
    pos_s = n_past + jnp.arange(dec_seq, dtype=jnp.int32)
    yp, rp = layer(x_prompt.reshape(batch * seq, d), pos_p, 512, attend_prompt)
    ys, rs = layer(x_sample.reshape(dec_batch * dec_seq, d), pos_s, 512, attend_sample)

    def rows(r, b, s):
        lat, kr, kn, vn = r
        return (lat.reshape(1, b, s, kv_lora), kr.reshape(1, b, s, rope_dim),
                kn.reshape(1, b, s, sb_heads, sb_dim), vn.reshape(1, b, s, sb_heads, sb_dim))

    return (yp.reshape(batch, seq, d), ys.reshape(dec_batch, dec_seq, d)) + rows(rp, batch, seq) + rows(rs, dec_batch, dec_seq)
```

```python
import functools

import jax
import jax.numpy as jnp
from jax import lax
from jax.experimental import pallas as pl
from jax.experimental.pallas import tpu as pltpu

EPS = 1e-6
CHUNK = 64
ROPE_THETA = 10000.0
LANES = 128
F32 = jnp.float32
BF16 = jnp.bfloat16
VMEM_LIMIT = 56 * 1024 * 1024
SUBLANES = 8
ONES_ROWS = 2 * SUBLANES
LOG2E = 1.4426950408889634
DEAD_EXPONENT = -120.0
NORM_MARGIN = 1.02
MLA_LANE_GROUPS = 4


def _dot(a, b):
    return jnp.dot(a, b, preferred_element_type=F32)


def _dot_nt(a, b):
    return lax.dot_general(a, b, (((1,), (1,)), ((), ())), preferred_element_type=F32)


def _dot_tn(a, b):
    return lax.dot_general(a, b, (((0,), (0,)), ((), ())), preferred_element_type=F32)


def _rms(x, g):
    return x * lax.rsqrt(jnp.mean(x * x, axis=-1, keepdims=True) + EPS) * g


def _params(*sem):
    return pltpu.CompilerParams(dimension_semantics=sem, vmem_limit_bytes=VMEM_LIMIT)


def _ffn_kernel(x_ref, gpre_ref, wg_ref, wu_ref, wd_ref, gpost_ref, gfin_ref, o_ref, *, final, fc):
    x = x_ref[...]
    xn = _rms(x, gpre_ref[...]).astype(BF16)
    d_ff = wg_ref.shape[1]
    acc = jnp.zeros(x.shape, F32)
    for c in range(0, d_ff, fc):
        g = _dot(xn, wg_ref[:, c:c + fc])
        u = _dot(xn, wu_ref[:, c:c + fc])
        a = (g * jax.nn.sigmoid(g)) * u
        acc = acc + _dot(a.astype(BF16), wd_ref[c:c + fc, :])
    h = x + 0.5 * _rms(acc, gpost_ref[...])
    if final:
        h = _rms(h, gfin_ref[...])
    o_ref[...] = h


def _ffn(x, g_pre, wg, wu, wd, g_post, g_fin, *, final, tm):
    t, d = x.shape
    d_ff = wg.shape[1]
    fc = d_ff // 2 if (d_ff // 2) % LANES == 0 else d_ff
    const = lambda i: (0, 0)
    return pl.pallas_call(
        functools.partial(_ffn_kernel, final=final, fc=fc),
        grid=(t // tm,),
        in_specs=[
            pl.BlockSpec((tm, d), lambda i: (i, 0)),
            pl.BlockSpec((1, d), const),
            pl.BlockSpec((d, d_ff), const, pipeline_mode=pl.Buffered(1)),
            pl.BlockSpec((d, d_ff), const, pipeline_mode=pl.Buffered(1)),
            pl.BlockSpec((d_ff, d), const, pipeline_mode=pl.Buffered(1)),
            pl.BlockSpec((1, d), const),
            pl.BlockSpec((1, d), const),
        ],
        out_specs=pl.BlockSpec((tm, d), lambda i: (i, 0)),
        out_shape=jax.ShapeDtypeStruct((t, d), F32),
        compiler_params=_params("parallel"),
        name="ffn_final" if final else "ffn",
    )(x, g_pre, wg, wu, wd, g_post, g_fin)


def _mix_in_kernel(h_ref, gpre_ref, win_ref, gq_ref, wuq_ref, wuk_ref, gkv_ref, cos_ref, sin_ref, sel_ref,
                   lat_ref, kr_ref, kn_ref, vn_ref, qcat_ref, kcat_ref, latt_ref, qsb_ref, ksb_ref, vsb_ref, ksq_ref,
                   *, heads, q_lora, kv_lora, rope_dim, sbw, mla_scale, sb_scale):
    u = _rms(h_ref[...], gpre_ref[...]).astype(BF16)
    proj = _dot(u, win_ref[...])
    cosp = cos_ref[...]
    sinp = sin_ref[...]
    o1 = q_lora
    o2 = o1 + kv_lora
    o3 = o2 + LANES

    def rope(grp):
        return grp * cosp + pltpu.roll(grp, LANES - rope_dim, 1) * sinp

    latent = _rms(proj[:, o1:o2], gkv_ref[...])
    krope = rope(proj[:, o2:o3])
    lat_ref[...] = latent
    kr_ref[...] = krope[:, :rope_dim]
    k_sb = proj[:, o3 + sbw:o3 + 2 * sbw]
    v_sb = proj[:, o3 + 2 * sbw:o3 + 3 * sbw]
    kn_ref[...] = k_sb
    vn_ref[...] = v_sb
    k_bf = k_sb.astype(BF16)
    ksb_ref[...] = k_bf
    vsb_ref[...] = v_sb.astype(BF16)
    qsb_ref[...] = (proj[:, o3:o3 + sbw] * sb_scale).astype(BF16)
    kcat_ref[...] = jnp.concatenate([latent, krope], axis=1).astype(BF16)
    latt_ref[...] = jnp.concatenate([latent.T, jnp.ones((ONES_ROWS, latent.shape[0]), F32)], axis=0).astype(BF16)
    k_f = k_bf.astype(F32)
    ksq_ref[...] = _dot((k_f * k_f).astype(BF16), sel_ref[...])

    c_q = _rms(proj[:, :o1], gq_ref[...]).astype(BF16)
    q = _dot(c_q, wuq_ref[...])
    n_nope = wuk_ref.shape[0]
    q_lat = _dot(q[:, :n_nope].astype(BF16), wuk_ref[...])
    for hd in range(heads):
        ql = q_lat[:, hd * kv_lora:(hd + 1) * kv_lora] * mla_scale
        qr = rope(q[:, n_nope + hd * LANES:n_nope + (hd + 1) * LANES]) * mla_scale
        qcat_ref[hd] = jnp.concatenate([ql, qr], axis=1).astype(BF16)


def _mix_in(h, g_pre, w_in_ext, g_q, w_uq_ext, w_uk_bd, g_kv, cosp, sinp, *, tm, dims):
    t, d = h.shape
    heads, q_lora, kv_lora, rope_dim, sbw, sb_heads, mla_scale, sb_scale = dims
    n_tab = cosp.shape[0] // tm
    const = lambda i: (0, 0)
    row = lambda i: (i, 0)
    tab = lambda i: (i % n_tab, 0)
    out_shape = (
        jax.ShapeDtypeStruct((t, kv_lora), F32),
        jax.ShapeDtypeStruct((t, rope_dim), F32),
        jax.ShapeDtypeStruct((t, sbw), F32),
        jax.ShapeDtypeStruct((t, sbw), F32),
        jax.ShapeDtypeStruct((heads, t, 2 * LANES), BF16),
        jax.ShapeDtypeStruct((t, 2 * LANES), BF16),
        jax.ShapeDtypeStruct((kv_lora + ONES_ROWS, t), BF16),
        jax.ShapeDtypeStruct((t, sbw), BF16),
        jax.ShapeDtypeStruct((t, sbw), BF16),
        jax.ShapeDtypeStruct((t, sbw), BF16),
        jax.ShapeDtypeStruct((t, LANES), F32),
    )
    out_specs = (
        pl.BlockSpec((tm, kv_lora), row),
        pl.BlockSpec((tm, rope_dim), row),
        pl.BlockSpec((tm, sbw), row),
        pl.BlockSpec((tm, sbw), row),
        pl.BlockSpec((heads, tm, 2 * LANES), lambda i: (0, i, 0)),
        pl.BlockSpec((tm, 2 * LANES), row),
        pl.BlockSpec((kv_lora + ONES_ROWS, tm), lambda i: (0, i)),
        pl.BlockSpec((tm, sbw), row),
        pl.BlockSpec((tm, sbw), row),
        pl.BlockSpec((tm, sbw), row),
        pl.BlockSpec((tm, LANES), row),
    )
    sel = (lax.broadcasted_iota(jnp.int32, (sbw, LANES), 0) // (sbw // sb_heads)
           == lax.broadcasted_iota(jnp.int32, (sbw, LANES), 1)).astype(BF16)
    return pl.pallas_call(
        functools.partial(_mix_in_kernel, heads=heads, q_lora=q_lora, kv_lora=kv_lora,
                          rope_dim=rope_dim, sbw=sbw, mla_scale=mla_scale, sb_scale=sb_scale),
        grid=(t // tm,),
        in_specs=[
            pl.BlockSpec((tm, d), row),
            pl.BlockSpec((1, d), const),
            pl.BlockSpec(w_in_ext.shape, const),
            pl.BlockSpec((1, q_lora), const),
            pl.BlockSpec(w_uq_ext.shape, const),
            pl.BlockSpec(w_uk_bd.shape, const),
            pl.BlockSpec((1, kv_lora), const),
            pl.BlockSpec((tm, LANES), tab),
            pl.BlockSpec((tm, LANES), tab),
            pl.BlockSpec((sbw, LANES), const),
        ],
        out_specs=out_specs,
        out_shape=out_shape,
        compiler_params=_params("parallel"),
        name="mix_in",
    )(h, g_pre, w_in_ext, g_q, w_uq_ext, w_uk_bd, g_kv, cosp, sinp, sel)


def _mla_scores(qs, ks):
    return sum(_dot_nt(kk, qq) for qq, kk in zip(qs, ks))


def _mla_probs(s, m_ref, visible):
    if visible is not None:
        s = jnp.where(visible, s, -jnp.inf)
    m_prev = m_ref[...]
    m_new = jnp.maximum(m_prev, jnp.max(s, axis=0, keepdims=True))
    m_ref[...] = m_new
    return jnp.exp2(m_prev - m_new), jnp.exp2(s - m_new)


def _mla_accumulate(alpha, p, v, acc_ref, v_transposed):
    pb = p.astype(BF16)
    if v_transposed:
        pv = _dot(v, pb)
    else:
        l = jnp.sum(p, axis=0, keepdims=True)
        pv = jnp.concatenate([_dot_tn(v, pb), jnp.broadcast_to(l, (ONES_ROWS, l.shape[1]))], axis=0)
    acc_ref[...] = alpha * acc_ref[...] + pv


def _mla_update(s, v, m_ref, acc_ref, visible, v_transposed):
    alpha, p = _mla_probs(s, m_ref, visible)
    _mla_accumulate(alpha, p, v, acc_ref, v_transposed)


def _own_block_rms_t(x, g_col, head_rows, tq):
    r = lax.broadcasted_iota(jnp.int32, x.shape, 0) // head_rows
    c = lax.broadcasted_iota(jnp.int32, x.shape, 1) // tq
    o = jnp.where(r == c, x, 0.0)
    inv = lax.rsqrt(jnp.sum(o * o, axis=0, keepdims=True) / head_rows + EPS)
    o_t = (o * inv * g_col).T
    out = o_t[:tq]
    for h in range(1, x.shape[1] // tq):
        out = out + o_t[h * tq:(h + 1) * tq]
    return out


def _mla_finish(acc_ref, wuvt_ref, g_ref, o_ref, heads, tq):
    c_dim = acc_ref.shape[0] - ONES_ROWS
    v_dim = wuvt_ref.shape[1]
    o_lat = (acc_ref[:c_dim, :] / acc_ref[c_dim:c_dim + 1, :]).astype(BF16)
    if tq % LANES == 0:
        outs = []
        for hd in range(heads):
            o = _dot(wuvt_ref[hd], o_lat[:, hd * tq:(hd + 1) * tq])
            outs.append(o * lax.rsqrt(jnp.mean(o * o, axis=0, keepdims=True) + EPS) * g_ref[hd])
        o_t = jnp.concatenate(outs, axis=0).T
    else:
        full = _dot(wuvt_ref[...].reshape(heads * v_dim, c_dim), o_lat)
        o_t = _own_block_rms_t(full, g_ref[...].reshape(heads * v_dim, 1), v_dim, tq)
    o_ref[...] = o_t.astype(BF16)


def _mla_init(m_ref, acc_ref):
    m_ref[...] = jnp.full(m_ref.shape, -jnp.inf, F32)
    acc_ref[...] = jnp.zeros(acc_ref.shape, F32)


def _mla_prompt_kernel(q_ref, k_ref, vt_ref, wuvt_ref, g_ref, o_ref, m_ref, acc_ref, *, heads, tq, tk):
    j = pl.program_id(1)
    q0 = j * tq
    q = q_ref[...].reshape(heads * tq, q_ref.shape[2])
    _mla_init(m_ref, acc_ref)
    n_full = q0 // tk

    rg = heads * tq // MLA_LANE_GROUPS

    def block(kb, visible):
        k0 = pl.multiple_of(kb * tk, tk)
        kblk = k_ref[pl.ds(k0, tk), :]
        vt = vt_ref[:, pl.ds(k0, tk)]
        lanes = [slice(g * rg, (g + 1) * rg) for g in range(MLA_LANE_GROUPS)]
        s, ap = {}, {}
        for step in range(MLA_LANE_GROUPS + 2):
            if step < MLA_LANE_GROUPS:
                s[step] = _dot_nt(kblk, q[lanes[step]])
            g = step - 1
            if 0 <= g < MLA_LANE_GROUPS:
                ap[g] = _mla_probs(s.pop(g), m_ref.at[:, lanes[g]], None if visible is None else visible[:, lanes[g]])
            g = step - 2
            if 0 <= g < MLA_LANE_GROUPS:
                _mla_accumulate(*ap.pop(g), vt, acc_ref.at[:, lanes[g]], True)

    def full_block(kb, carry):
        block(kb, None)
        return carry

    lax.fori_loop(0, n_full, full_block, 0)
    kpos = n_full * tk + lax.broadcasted_iota(jnp.int32, (tk, heads * tq), 0)
    qpos = q0 + lax.broadcasted_iota(jnp.int32, (tk, heads * tq), 1) % tq
    block(n_full, (kpos // CHUNK) <= (qpos // CHUNK))
    _mla_finish(acc_ref, wuvt_ref, g_ref, o_ref, heads, tq)


def _mla_scratch(rows, c_dim):
    return [pltpu.VMEM((1, rows), F32), pltpu.VMEM((c_dim + ONES_ROWS, rows), F32)]


def _mla_prompt(qcat, kcat, lat_t, w_uv_t, g_out, *, batch, seq, tq, tk):
    heads, t, dk = qcat.shape
    v_dim, c_dim = w_uv_t.shape[1], w_uv_t.shape[2]
    assert tk % tq == 0 and tq % CHUNK == 0 and seq % tk == 0 and tq % LANES == 0
    nq = seq // tq
    return pl.pallas_call(
        functools.partial(_mla_prompt_kernel, heads=heads, tq=tq, tk=tk),
        grid=(batch, nq),
        in_specs=[
            pl.BlockSpec((heads, tq, dk), lambda b, j: (0, b * nq + j, 0)),
            pl.BlockSpec((seq, dk), lambda b, j: (b, 0)),
            pl.BlockSpec((lat_t.shape[0], seq), lambda b, j: (0, b)),
            pl.BlockSpec(w_uv_t.shape, lambda b, j: (0, 0, 0)),
            pl.BlockSpec(g_out.shape, lambda b, j: (0, 0, 0)),
        ],
        out_specs=pl.BlockSpec((tq, heads * v_dim), lambda b, j: (b * nq + j, 0)),
        out_shape=jax.ShapeDtypeStruct((t, heads * v_dim), BF16),
        scratch_shapes=_mla_scratch(heads * tq, c_dim),
        compiler_params=_params("parallel", "arbitrary"),
        name="mla_prompt",
    )(qcat, kcat, lat_t, w_uv_t, g_out)


def _mla_sample_kernel(q_ref, lat_ref, kr_ref, knew_ref, wuvt_ref, g_ref, o_ref, m_ref, acc_ref,
                       *, heads, tq, tk, c_dim, rope_dim):
    q = q_ref[...].reshape(heads * tq, q_ref.shape[2])
    q_lat = q[:, :c_dim]
    q_rope = q[:, c_dim:c_dim + rope_dim]
    _mla_init(m_ref, acc_ref)
    n_blk = lat_ref.shape[0] // tk

    def latent(kb):
        return lat_ref[pl.ds(pl.multiple_of(kb * tk, tk), tk), :].astype(BF16)

    def scores(kb):
        kr = kr_ref[pl.ds(pl.multiple_of(kb * tk, tk), tk), :].astype(BF16)
        return _mla_scores([q_lat, q_rope], [latent(kb), kr])

    def past_block(kb, s):
        s_next = scores(kb + 1)
        _mla_update(s, latent(kb), m_ref, acc_ref, None, False)
        return s_next

    s = lax.fori_loop(0, n_blk - 1, past_block, scores(0))
    knew = knew_ref[...]
    s_new = _mla_scores([q], [knew])
    _mla_update(s, latent(n_blk - 1), m_ref, acc_ref, None, False)
    _mla_update(s_new, knew[:, :c_dim], m_ref, acc_ref, None, False)
    _mla_finish(acc_ref, wuvt_ref, g_ref, o_ref, heads, tq)


def _mla_sample(qcat, cache_lat, cache_kr, kcat, w_uv_t, g_out, *, tk):
    heads, t, dk = qcat.shape
    batch, n_past, c_dim = cache_lat.shape
    rope_dim = cache_kr.shape[2]
    tq = qh = t // batch
    v_dim = w_uv_t.shape[1]
    assert n_past % CHUNK == 0 and qh <= CHUNK and n_past % tk == 0 and (heads * tq) % LANES == 0
    return pl.pallas_call(
        functools.partial(_mla_sample_kernel, heads=heads, tq=tq, tk=tk, c_dim=c_dim, rope_dim=rope_dim),
        grid=(batch,),
        in_specs=[
            pl.BlockSpec((heads, qh, dk), lambda b: (0, b, 0)),
            pl.BlockSpec((None, n_past, c_dim), lambda b: (b, 0, 0)),
            pl.BlockSpec((None, n_past, rope_dim), lambda b: (b, 0, 0)),
            pl.BlockSpec((qh, dk), lambda b: (b, 0)),
            pl.BlockSpec(w_uv_t.shape, lambda b: (0, 0, 0)),
            pl.BlockSpec(g_out.shape, lambda b: (0, 0, 0)),
        ],
        out_specs=pl.BlockSpec((qh, heads * v_dim), lambda b: (b, 0)),
        out_shape=jax.ShapeDtypeStruct((t, heads * v_dim), BF16),
        scratch_shapes=_mla_scratch(heads * tq, c_dim),
        compiler_params=_params("parallel"),
        name="mla_sample",
    )(qcat, cache_lat, cache_kr, kcat, w_uv_t, g_out)


def _sb_update(q2, kblk, vblk, triu, carry, acc, before, skip_dead=False):
    z = _dot_nt(kblk, q2)
    sub = triu.shape[0]

    def live():
        sp = jnp.maximum(z, 0.0) + jnp.log(1.0 + jnp.exp(-jnp.abs(z)))
        if before is not None:
            sp = jnp.where(before, sp, 0.0)
        hi = sp.astype(BF16)
        lo = (sp - hi.astype(F32)).astype(BF16)
        cs, run = [], carry
        for r0 in reversed(range(0, z.shape[0], sub)):
            cs.insert(0, _dot(triu, hi[r0:r0 + sub]) + _dot(triu, lo[r0:r0 + sub]) + run)
            run = cs[0][0:1, :]
        c = cs[0] if len(cs) == 1 else jnp.concatenate(cs, axis=0)
        a = jnp.exp(z - c)
        if before is not None:
            a = jnp.where(before, a, 0.0)
        return run, acc + _dot_tn(vblk() if callable(vblk) else vblk, a.astype(BF16))

    if not skip_dead:
        return live()
    return lax.cond(jnp.max(z - carry) > DEAD_EXPONENT, live, lambda: (carry, acc))


def _sb_queries(q, head_dim):
    lane = lax.broadcasted_iota(jnp.int32, (1, q.shape[1]), 1) // head_dim
    return jnp.concatenate([jnp.where(lane == h, q, jnp.zeros_like(q)) for h in range(q.shape[1] // head_dim)], axis=0)


def _sb_before(tk, tq, n, first_query_row=0):
    key = lax.broadcasted_iota(jnp.int32, (tk, n), 0) - first_query_row
    qry = lax.broadcasted_iota(jnp.int32, (tk, n), 1) % tq
    return key < qry


def _sb_finish(acc, g_ref, o_ref, head_dim, tq):
    o_ref[...] = _own_block_rms_t(acc, g_ref[...], head_dim, tq).astype(BF16)


def _sb_prompt_kernel(kmax_ref, q_ref, k_ref, v_ref, tri_ref, g_ref, o_ref, *, tq, head_dim):
    b, p, j = pl.program_id(0), pl.program_id(1), pl.program_id(2)
    n_blk = pl.num_programs(2)
    q2 = _sb_queries(q_ref[...], head_dim)
    width = q_ref.shape[1]
    n = q2.shape[0]
    hpg = width // head_dim
    triu = tri_ref[...]
    zero = (jnp.zeros((1, n), F32), jnp.zeros((width, n), F32))

    def own_block():
        return _sb_update(q2, k_ref[pl.ds(0, tq), :], v_ref[pl.ds(0, tq), :], triu, *zero, _sb_before(tq, tq, n))

    def own_and_previous_block():
        k0 = pl.multiple_of((j - 1) * tq, tq)
        return _sb_update(q2, k_ref[pl.ds(k0, 2 * tq), :], v_ref[pl.ds(k0, 2 * tq), :], triu, *zero,
                          _sb_before(2 * tq, tq, n, first_query_row=tq))

    carry, acc = lax.cond(j > 0, own_and_previous_block, own_block)
    q2f = q2.astype(F32)
    qn = jnp.sqrt(_dot_nt(jnp.ones((SUBLANES, width), BF16), (q2f * q2f).astype(BF16))[0:1, :]) * NORM_MARGIN
    lane_head = lax.broadcasted_iota(jnp.int32, (1, n), 1) // tq

    def alive(kb, carry):
        base = (b * n_blk + jnp.maximum(kb, 0)) * (hpg * pl.num_programs(1)) + p * hpg
        kmax = jnp.zeros((1, n), F32)
        for h in range(hpg):
            kmax = jnp.where(lane_head == h, kmax_ref[base + h], kmax)
        return jnp.logical_and(kb >= 0, jnp.max(qn * kmax - carry) > DEAD_EXPONENT)

    def cond(state):
        return state[1]

    def earlier_block(state):
        kb, _, carry, acc = state
        k0 = pl.multiple_of(kb * tq, tq)
        carry, acc = _sb_update(q2, k_ref[pl.ds(k0, tq), :], v_ref[pl.ds(k0, tq), :], triu, carry, acc, None)
        return kb - 1, alive(kb - 1, carry), carry, acc

    state = lax.while_loop(cond, earlier_block, (j - 2, alive(j - 2, carry), carry, acc))
    _sb_finish(state[3], g_ref, o_ref, head_dim, tq)


def _triu(n):
    r = lax.broadcasted_iota(jnp.int32, (n, n), 0)
    c = lax.broadcasted_iota(jnp.int32, (n, n), 1)
    return (c >= r).astype(BF16)


def _sb_prompt(q_sb, k_sb, v_sb, ksq, g_col, *, batch, seq, tq, head_dim):
    t, sbw = q_sb.shape
    nq = seq // tq
    n_grp = sbw // LANES
    n_heads = sbw // head_dim
    blk_max = jnp.max(ksq[:, :n_heads].reshape(batch, nq, tq, n_heads), axis=2)
    kmax = (jnp.sqrt(lax.cummax(blk_max, axis=1)) * NORM_MARGIN).reshape(-1)
    return pl.pallas_call(
        functools.partial(_sb_prompt_kernel, tq=tq, head_dim=head_dim),
        grid_spec=pltpu.PrefetchScalarGridSpec(
            num_scalar_prefetch=1,
            grid=(batch, n_grp, nq),
            in_specs=[
                pl.BlockSpec((tq, LANES), lambda b, p, j, kmax: (b * nq + j, p)),
                pl.BlockSpec((seq, LANES), lambda b, p, j, kmax: (b, p)),
                pl.BlockSpec((seq, LANES), lambda b, p, j, kmax: (b, p)),
                pl.BlockSpec((tq, tq), lambda b, p, j, kmax: (0, 0)),
                pl.BlockSpec((LANES, 1), lambda b, p, j, kmax: (p, 0)),
            ],
            out_specs=pl.BlockSpec((tq, LANES), lambda b, p, j, kmax: (b * nq + j, p)),
        ),
        out_shape=jax.ShapeDtypeStruct((t, sbw), BF16),
        compiler_params=_params("parallel", "parallel", "arbitrary"),
        name="sb_prompt",
    )(kmax, q_sb, k_sb, v_sb, _triu(tq), g_col)


def _sb_sample_kernel(q_ref, kc_ref, vc_ref, kn_ref, vn_ref, tri_ref, g_ref, o_ref, *, tk, head_dim):
    tq, width = q_ref.shape
    q2 = _sb_queries(q_ref[...], head_dim)
    n = q2.shape[0]
    r = lax.broadcasted_iota(jnp.int32, (tq, tq), 0)
    c = lax.broadcasted_iota(jnp.int32, (tq, tq), 1)
    state = _sb_update(q2, kn_ref[...], vn_ref[...], (c >= r).astype(BF16),
                       jnp.zeros((1, n), F32), jnp.zeros((width, n), F32), _sb_before(tq, tq, n))
    triu = tri_ref[...]
    n_blk = kc_ref.shape[0] // tk

    def cached_block(kb, state, skip_dead):
        k0 = pl.multiple_of(kb * tk, tk)
        return _sb_update(q2, kc_ref[pl.ds(k0, tk), :].astype(BF16), lambda: vc_ref[pl.ds(k0, tk), :].astype(BF16),
                          triu, state[0], state[1], None, skip_dead=skip_dead)

    state = cached_block(n_blk - 1, state, False)
    rest = (n_blk - 1) * tk
    z_rest = _dot_nt(kc_ref[:rest, :].astype(BF16), q2)

    def walk():
        return lax.fori_loop(0, n_blk - 1, lambda i, st: cached_block(n_blk - 2 - i, st, True), state)

    state = lax.cond(jnp.max(z_rest - state[0]) > DEAD_EXPONENT, walk, lambda: state)
    _sb_finish(state[1], g_ref, o_ref, head_dim, tq)


def _sb_sample(q_sb, cache_k, cache_v, k_sb, v_sb, g_col, *, tk, head_dim):
    t, sbw = q_sb.shape
    batch, n_past, _ = cache_k.shape
    tq = t // batch
    n_grp = sbw // LANES
    assert n_past % tk == 0
    new = pl.BlockSpec((tq, LANES), lambda b, p: (b, p))
    past = pl.BlockSpec((None, n_past, LANES), lambda b, p: (b, 0, p))
    return pl.pallas_call(
        functools.partial(_sb_sample_kernel, tk=tk, head_dim=head_dim),
        grid=(batch, n_grp),
        in_specs=[new, past, past, new, new,
                  pl.BlockSpec((tk, tk), lambda b, p: (0, 0)),
                  pl.BlockSpec((LANES, 1), lambda b, p: (p, 0))],
        out_specs=new,
        out_shape=jax.ShapeDtypeStruct((t, sbw), BF16),
        compiler_params=_params("parallel", "parallel"),
        name="sb_sample",
    )(q_sb, cache_k, cache_v, k_sb, v_sb, _triu(tk), g_col)


def _out_proj_kernel(oa_ref, ob_ref, h_ref, wa_ref, wb_ref, g_ref, o_ref):
    m = _dot(oa_ref[...], wa_ref[...]) + _dot(ob_ref[...], wb_ref[...])
    o_ref[...] = h_ref[...] + _rms(m, g_ref[...])


def _out_proj(o_a, o_b, h, w_a, w_b, g_post, *, tm):
    t, d = h.shape
    row = lambda i: (i, 0)
    const = lambda i: (0, 0)
    return pl.pallas_call(
        _out_proj_kernel,
        grid=(t // tm,),
        in_specs=[
            pl.BlockSpec((tm, o_a.shape[1]), row),
            pl.BlockSpec((tm, o_b.shape[1]), row),
            pl.BlockSpec((tm, d), row),
            pl.BlockSpec(w_a.shape, const),
            pl.BlockSpec(w_b.shape, const),
            pl.BlockSpec((1, d), const),
        ],
        out_specs=pl.BlockSpec((tm, d), row),
        out_shape=jax.ShapeDtypeStruct((t, d), F32),
        compiler_params=_params("parallel"),
        name="out_proj",
    )(o_a, o_b, h, w_a, w_b, g_post)


def _rot_cols(w):
    half = w.shape[-1] // 2
    return jnp.concatenate([-w[..., half:], w[..., :half]], axis=-1)


def _rope_tables(pos, rope_dim, rows):
    half = rope_dim // 2
    inv_freq = ROPE_THETA ** (-jnp.arange(half, dtype=F32) / half)
    ang = pos.astype(F32)[:, None] * inv_freq[None, :]
    pad = jnp.zeros((pos.shape[0], LANES - rope_dim), F32)
    cosp = jnp.concatenate([jnp.cos(ang), jnp.cos(ang), pad], axis=1)
    sinp = jnp.concatenate([jnp.sin(ang), jnp.sin(ang), pad], axis=1)
    reps = max(1, rows // pos.shape[0])
    return jnp.tile(cosp, (reps, 1)), jnp.tile(sinp, (reps, 1))


def kernel(x_prompt, x_sample, cache_mla_latent, cache_mla_krope, cache_sb_k, cache_sb_v, g_pre_ff1, w_gate1, w_up1, w_down1, g_post_ff1, g_pre_mix, w_in, g_q, w_uq, g_kv, w_uk, w_uv, g_mla_out, g_sb_out, w_out, g_post_mix, g_pre_ff2, w_gate2, w_up2, w_down2, g_post_ff2, g_final):
    depth = w_in.shape[0]
    assert depth == 1
    batch, seq, d = x_prompt.shape
    dec_batch, dec_seq, _ = x_sample.shape
    n_past = cache_mla_latent.shape[2]
    _, q_lora, heads, qk_dim = w_uq.shape
    kv_lora, _, nope = w_uk.shape[1:]
    rope_dim = qk_dim - nope
    v_dim = w_uv.shape[3]
    sb_heads, sb_dim = g_sb_out.shape[1:]
    sbw = sb_heads * sb_dim
    mla_scale = float(qk_dim) ** -0.5
    sb_scale = float(sb_dim) ** -0.5
    assert kv_lora == LANES and 2 * rope_dim <= LANES and LANES % sb_dim == 0
    dims = (heads, q_lora, kv_lora, rope_dim, sbw, sb_heads, mla_scale * LOG2E, sb_scale)

    row = lambda g: g.reshape(1, -1).astype(F32)
    l = 0
    wi = w_in[l]
    i1 = q_lora + kv_lora
    i2 = i1 + rope_dim
    w_kr = wi[:, i1:i2]
    w_in_ext = jnp.concatenate(
        [wi[:, :i1], w_kr, _rot_cols(w_kr), jnp.zeros((d, LANES - 2 * rope_dim), F32), wi[:, i2:]], axis=1).astype(BF16)
    wq = w_uq[l]
    wq_rope = wq[:, :, nope:]
    wq_grp = jnp.concatenate(
        [wq_rope, _rot_cols(wq_rope), jnp.zeros((q_lora, heads, LANES - 2 * rope_dim), F32)], axis=2)
    w_uq_ext = jnp.concatenate(
        [wq[:, :, :nope].reshape(q_lora, heads * nope), wq_grp.reshape(q_lora, heads * LANES)], axis=1).astype(BF16)
    wk = jnp.transpose(w_uk[l], (1, 2, 0))
    eye = jnp.eye(heads, dtype=F32)
    w_uk_bd = (wk[:, :, None, :] * eye[:, None, :, None]).reshape(heads * nope, heads * kv_lora).astype(BF16)
    w_uv_t = jnp.transpose(w_uv[l], (1, 2, 0)).astype(BF16)
    w_out_a = w_out[l][:heads * v_dim].astype(BF16)
    w_out_b = w_out[l][heads * v_dim:].astype(BF16)
    ffn1 = (row(g_pre_ff1[l]), w_gate1[l].astype(BF16), w_up1[l].astype(BF16), w_down1[l].astype(BF16),
            row(g_post_ff1[l]), row(g_final[l]))
    ffn2 = (row(g_pre_ff2[l]), w_gate2[l].astype(BF16), w_up2[l].astype(BF16), w_down2[l].astype(BF16),
            row(g_post_ff2[l]), row(g_final[l]))
    g_mla = g_mla_out[l].reshape(heads, v_dim, 1).astype(F32)
    g_sb = g_sb_out[l].reshape(sbw, 1).astype(F32)

    def layer(x, pos, tm, attend):
        h = _ffn(x, *ffn1, final=False, tm=tm)
        cosp, sinp = _rope_tables(pos, rope_dim, tm)
        lat, kr, kn, vn, qcat, kcat, lat_t, q_sb, k_sb, v_sb, ksq = _mix_in(
            h, row(g_pre_mix[l]), w_in_ext, row(g_q[l]), w_uq_ext, w_uk_bd, row(g_kv[l]), cosp, sinp,
            tm=tm, dims=dims)
        o_mla, o_sb = attend(qcat, kcat, lat_t, q_sb, k_sb, v_sb, ksq)
        h = _out_proj(o_mla, o_sb, h, w_out_a, w_out_b, row(g_post_mix[l]), tm=tm)
        y = _ffn(h, *ffn2, final=True, tm=tm)
        return y, (lat, kr, kn, vn)

    def attend_prompt(qcat, kcat, lat_t, q_sb, k_sb, v_sb, ksq):
        o_mla = _mla_prompt(qcat, kcat, lat_t, w_uv_t, g_mla, batch=batch, seq=seq, tq=256, tk=512)
        o_sb = _sb_prompt(q_sb, k_sb, v_sb, ksq, g_sb, batch=batch, seq=seq, tq=256, head_dim=sb_dim)
        return o_mla, o_sb

    def attend_sample(qcat, kcat, lat_t, q_sb, k_sb, v_sb, ksq):
        o_mla = _mla_sample(qcat, cache_mla_latent[l], cache_mla_krope[l], kcat, w_uv_t, g_mla, tk=512)
        o_sb = _sb_sample(q_sb, cache_sb_k[l].reshape(dec_batch, n_past, sbw),
                          cache_sb_v[l].reshape(dec_batch, n_past, sbw), k_sb, v_sb, g_sb, tk=256, head_dim=sb_dim)
        return o_mla, o_sb

    pos_p = jnp.arange(seq, dtype=jnp.int32)
    pos_s = n_past + jnp.arange(dec_seq, dtype=jnp.int32)
    yp, rp = layer(x_prompt.reshape(batch * seq, d), pos_p, 512, attend_prompt)
    ys, rs = layer(x_sample.reshape(dec_batch * dec_seq, d), pos_s, 512, attend_sample)

    def rows(r, b, s):
        lat, kr, kn, vn = r
        return (lat.reshape(1, b, s, kv_lora), kr.reshape(1, b, s, rope_dim),
                kn.reshape(1, b, s, sb_heads, sb_dim), vn.reshape(1, b, s, sb_heads, sb_dim))

    return (yp.reshape(batch, seq, d), ys.reshape(dec_batch, dec_seq, d)) + rows(rp, batch, seq) + rows(rs, dec_batch, dec_seq)
```

```python
import functools

import jax
import jax.numpy as jnp
from jax import lax
from jax.experimental import pallas as pl
from jax.experimental.pallas import tpu as pltpu

EPS = 1e-6
CHUNK = 64
ROPE_THETA = 10000.0
LANES = 128
F32 = jnp.float32
BF16 = jnp.bfloat16
VMEM_LIMIT = 56 * 1024 * 1024
SUBLANES = 8
MXU_WIDTH = 256
ONES_ROWS = 2 * SUBLANES
LOG2E = 1.4426950408889634
DEAD_EXPONENT = -120.0
NORM_MARGIN = 1.02
MLA_GROUP_LANES = 512


def _dot(a, b):
    return jnp.dot(a, b, preferred_element_type=F32)


def _dot_nt(a, b):
    return lax.dot_general(a, b, (((1,), (1,)), ((), ())), preferred_element_type=F32)


def _dot_tn(a, b):
    return lax.dot_general(a, b, (((0,), (0,)), ((), ())), preferred_element_type=F32)


def _rms(x, g):
    return x * lax.rsqrt(jnp.mean(x * x, axis=-1, keepdims=True) + EPS) * g


def _params(*sem):
    return pltpu.CompilerParams(dimension_semantics=sem, vmem_limit_bytes=VMEM_LIMIT)


def _ffn_kernel(x_ref, gpre_ref, wg_ref, wu_ref, wd_ref, gpost_ref, gfin_ref, o_ref, *, final, fc):
    x = x_ref[...]
    xn = _rms(x, gpre_ref[...]).astype(BF16)
    d_ff = wg_ref.shape[1]
    acc = jnp.zeros(x.shape, F32)
    for c in range(0, d_ff, fc):
        g = _dot(xn, wg_ref[:, c:c + fc])
        u = _dot(xn, wu_ref[:, c:c + fc])
        a = (g * jax.nn.sigmoid(g)) * u
        acc = acc + _dot(a.astype(BF16), wd_ref[c:c + fc, :])
    h = x + 0.5 * _rms(acc, gpost_ref[...])
    if final:
        h = _rms(h, gfin_ref[...])
    o_ref[...] = h


def _ffn(x, g_pre, wg, wu, wd, g_post, g_fin, *, final, tm):
    t, d = x.shape
    d_ff = wg.shape[1]
    fc = MXU_WIDTH
    assert d_ff % fc == 0
    const = lambda i: (0, 0)
    return pl.pallas_call(
        functools.partial(_ffn_kernel, final=final, fc=fc),
        grid=(t // tm,),
        in_specs=[
            pl.BlockSpec((tm, d), lambda i: (i, 0)),
            pl.BlockSpec((1, d), const),
            pl.BlockSpec((d, d_ff), const, pipeline_mode=pl.Buffered(1)),
            pl.BlockSpec((d, d_ff), const, pipeline_mode=pl.Buffered(1)),
            pl.BlockSpec((d_ff, d), const, pipeline_mode=pl.Buffered(1)),
            pl.BlockSpec((1, d), const),
            pl.BlockSpec((1, d), const),
        ],
        out_specs=pl.BlockSpec((tm, d), lambda i: (i, 0)),
        out_shape=jax.ShapeDtypeStruct((t, d), F32),
        compiler_params=_params("parallel"),
        name="ffn_final" if final else "ffn",
    )(x, g_pre, wg, wu, wd, g_post, g_fin)


def _mix_in_kernel(h_ref, gpre_ref, win_ref, gq_ref, wuq_ref, wuk_ref, gkv_ref, cos_ref, sin_ref, sel_ref,
                   lat_ref, kr_ref, kn_ref, vn_ref, qcat_ref, kcat_ref, latt_ref, qsb_ref, ksb_ref, vsb_ref, ksq_ref,
                   *, heads, q_lora, kv_lora, rope_dim, sbw, mla_scale, sb_scale):
    u = _rms(h_ref[...], gpre_ref[...]).astype(BF16)
    proj = _dot(u, win_ref[...])
    cosp = cos_ref[...]
    sinp = sin_ref[...]
    o1 = q_lora
    o2 = o1 + kv_lora
    o3 = o2 + LANES

    def rope(grp):
        return grp * cosp + pltpu.roll(grp, LANES - rope_dim, 1) * sinp

    latent = _rms(proj[:, o1:o2], gkv_ref[...])
    krope = rope(proj[:, o2:o3])
    lat_ref[...] = latent
    kr_ref[...] = krope[:, :rope_dim]
    k_sb = proj[:, o3 + sbw:o3 + 2 * sbw]
    v_sb = proj[:, o3 + 2 * sbw:o3 + 3 * sbw]
    kn_ref[...] = k_sb
    vn_ref[...] = v_sb
    k_bf = k_sb.astype(BF16)
    ksb_ref[...] = k_bf
    vsb_ref[...] = v_sb.astype(BF16)
    qsb_ref[...] = (proj[:, o3:o3 + sbw] * sb_scale).astype(BF16)
    kcat_ref[...] = jnp.concatenate([latent, krope], axis=1).astype(BF16)
    latt_ref[...] = jnp.concatenate([latent.T, jnp.ones((ONES_ROWS, latent.shape[0]), F32)], axis=0).astype(BF16)
    k_f = k_bf.astype(F32)
    ksq_ref[...] = _dot((k_f * k_f).astype(BF16), sel_ref[...])

    c_q = _rms(proj[:, :o1], gq_ref[...]).astype(BF16)
    q = _dot(c_q, wuq_ref[...])
    n_nope = wuk_ref.shape[0]
    q_lat = _dot(q[:, :n_nope].astype(BF16), wuk_ref[...])
    for hd in range(heads):
        ql = q_lat[:, hd * kv_lora:(hd + 1) * kv_lora] * mla_scale
        qr = rope(q[:, n_nope + hd * LANES:n_nope + (hd + 1) * LANES]) * mla_scale
        qcat_ref[hd] = jnp.concatenate([ql, qr], axis=1).astype(BF16)


def _mix_in(h, g_pre, w_in_ext, g_q, w_uq_ext, w_uk_bd, g_kv, cosp, sinp, *, tm, dims):
    t, d = h.shape
    heads, q_lora, kv_lora, rope_dim, sbw, sb_heads, mla_scale, sb_scale = dims
    n_tab = cosp.shape[0] // tm
    const = lambda i: (0, 0)
    row = lambda i: (i, 0)
    tab = lambda i: (i % n_tab, 0)
    out_shape = (
        jax.ShapeDtypeStruct((t, kv_lora), F32),
        jax.ShapeDtypeStruct((t, rope_dim), F32),
        jax.ShapeDtypeStruct((t, sbw), F32),
        jax.ShapeDtypeStruct((t, sbw), F32),
        jax.ShapeDtypeStruct((heads, t, 2 * LANES), BF16),
        jax.ShapeDtypeStruct((t, 2 * LANES), BF16),
        jax.ShapeDtypeStruct((kv_lora + ONES_ROWS, t), BF16),
        jax.ShapeDtypeStruct((t, sbw), BF16),
        jax.ShapeDtypeStruct((t, sbw), BF16),
        jax.ShapeDtypeStruct((t, sbw), BF16),
        jax.ShapeDtypeStruct((t, LANES), F32),
    )
    out_specs = (
        pl.BlockSpec((tm, kv_lora), row),
        pl.BlockSpec((tm, rope_dim), row),
        pl.BlockSpec((tm, sbw), row),
        pl.BlockSpec((tm, sbw), row),
        pl.BlockSpec((heads, tm, 2 * LANES), lambda i: (0, i, 0)),
        pl.BlockSpec((tm, 2 * LANES), row),
        pl.BlockSpec((kv_lora + ONES_ROWS, tm), lambda i: (0, i)),
        pl.BlockSpec((tm, sbw), row),
        pl.BlockSpec((tm, sbw), row),
        pl.BlockSpec((tm, sbw), row),
        pl.BlockSpec((tm, LANES), row),
    )
    sel = (lax.broadcasted_iota(jnp.int32, (sbw, LANES), 0) // (sbw // sb_heads)
           == lax.broadcasted_iota(jnp.int32, (sbw, LANES), 1)).astype(BF16)
    return pl.pallas_call(
        functools.partial(_mix_in_kernel, heads=heads, q_lora=q_lora, kv_lora=kv_lora,
                          rope_dim=rope_dim, sbw=sbw, mla_scale=mla_scale, sb_scale=sb_scale),
        grid=(t // tm,),
        in_specs=[
            pl.BlockSpec((tm, d), row),
            pl.BlockSpec((1, d), const),
            pl.BlockSpec(w_in_ext.shape, const),
            pl.BlockSpec((1, q_lora), const),
            pl.BlockSpec(w_uq_ext.shape, const),
            pl.BlockSpec(w_uk_bd.shape, const),
            pl.BlockSpec((1, kv_lora), const),
            pl.BlockSpec((tm, LANES), tab),
            pl.BlockSpec((tm, LANES), tab),
            pl.BlockSpec((sbw, LANES), const),
        ],
        out_specs=out_specs,
        out_shape=out_shape,
        compiler_params=_params("parallel"),
        name="mix_in",
    )(h, g_pre, w_in_ext, g_q, w_uq_ext, w_uk_bd, g_kv, cosp, sinp, sel)


def _mla_scores(qs, ks):
    return sum(_dot_nt(kk, qq) for qq, kk in zip(qs, ks))


def _mla_probs(s, m_ref, visible):
    if visible is not None:
        s = jnp.where(visible, s, -jnp.inf)
    m_prev = m_ref[...]
    m_new = jnp.maximum(m_prev, jnp.max(s, axis=0, keepdims=True))
    m_ref[...] = m_new
    return jnp.exp2(m_prev - m_new), jnp.exp2(s - m_new)


def _mla_accumulate(alpha, p, v, acc_ref, v_transposed):
    pb = p.astype(BF16)
    if v_transposed:
        pv = _dot(v, pb)
    else:
        l = jnp.sum(p, axis=0, keepdims=True)
        pv = jnp.concatenate([_dot_tn(v, pb), jnp.broadcast_to(l, (ONES_ROWS, l.shape[1]))], axis=0)
    acc_ref[...] = alpha * acc_ref[...] + pv


def _mla_update(s, v, m_ref, acc_ref, visible, v_transposed):
    alpha, p = _mla_probs(s, m_ref, visible)
    _mla_accumulate(alpha, p, v, acc_ref, v_transposed)


def _own_block_rms_t(x, g_col, head_rows, tq):
    r = lax.broadcasted_iota(jnp.int32, x.shape, 0) // head_rows
    c = lax.broadcasted_iota(jnp.int32, x.shape, 1) // tq
    o = jnp.where(r == c, x, 0.0)
    inv = lax.rsqrt(jnp.sum(o * o, axis=0, keepdims=True) / head_rows + EPS)
    o_t = (o * inv * g_col).T
    out = o_t[:tq]
    for h in range(1, x.shape[1] // tq):
        out = out + o_t[h * tq:(h + 1) * tq]
    return out


def _mla_finish(acc_ref, wuvt_ref, g_ref, o_ref, heads, tq):
    c_dim = acc_ref.shape[0] - ONES_ROWS
    v_dim = wuvt_ref.shape[1]
    o_lat = (acc_ref[:c_dim, :] / acc_ref[c_dim:c_dim + 1, :]).astype(BF16)
    if tq % LANES == 0:
        outs = []
        for hd in range(heads):
            o = _dot(wuvt_ref[hd], o_lat[:, hd * tq:(hd + 1) * tq])
            outs.append(o * lax.rsqrt(jnp.mean(o * o, axis=0, keepdims=True) + EPS) * g_ref[hd])
        o_t = jnp.concatenate(outs, axis=0).T
    else:
        full = _dot(wuvt_ref[...].reshape(heads * v_dim, c_dim), o_lat)
        o_t = _own_block_rms_t(full, g_ref[...].reshape(heads * v_dim, 1), v_dim, tq)
    o_ref[...] = o_t.astype(BF16)


def _mla_init(m_ref, acc_ref):
    m_ref[...] = jnp.full(m_ref.shape, -jnp.inf, F32)
    acc_ref[...] = jnp.zeros(acc_ref.shape, F32)


def _mla_prompt_kernel(q_ref, k_ref, vt_ref, wuvt_ref, g_ref, o_ref, m_ref, acc_ref, *, heads, tq, tk):
    j = pl.program_id(1)
    q0 = j * tq
    q = q_ref[...].reshape(heads * tq, q_ref.shape[2])
    _mla_init(m_ref, acc_ref)
    n_full = q0 // tk

    def blocks(todo):
        lanes = [slice(r0, r0 + MLA_GROUP_LANES) for r0 in range(0, heads * tq, MLA_GROUP_LANES)]
        units = [(kb, vis, ln) for kb, vis in todo for ln in lanes]

        def rows(kb):
            return pl.ds(pl.multiple_of(kb * tk, tk), tk)

        s, ap = {}, {}
        for step in range(len(units) + 2):
            if step < len(units):
                kb, _, ln = units[step]
                s[step] = _dot_nt(k_ref[rows(kb), :], q[ln])
            if 0 <= step - 1 < len(units):
                kb, vis, ln = units[step - 1]
                ap[step - 1] = _mla_probs(s.pop(step - 1), m_ref.at[:, ln], None if vis is None else vis[:, ln])
            if 0 <= step - 2 < len(units):
                kb, _, ln = units[step - 2]
                _mla_accumulate(*ap.pop(step - 2), vt_ref[:, rows(kb)], acc_ref.at[:, ln], True)

    def full_block(kb, carry):
        blocks([(kb, None)])
        return carry

    lax.fori_loop(0, n_full, full_block, 0)
    kpos = n_full * tk + lax.broadcasted_iota(jnp.int32, (tk, heads * tq), 0)
    qpos = q0 + lax.broadcasted_iota(jnp.int32, (tk, heads * tq), 1) % tq
    blocks([(n_full, (kpos // CHUNK) <= (qpos // CHUNK))])
    _mla_finish(acc_ref, wuvt_ref, g_ref, o_ref, heads, tq)


def _mla_scratch(rows, c_dim):
    return [pltpu.VMEM((1, rows), F32), pltpu.VMEM((c_dim + ONES_ROWS, rows), F32)]


def _mla_prompt(qcat, kcat, lat_t, w_uv_t, g_out, *, batch, seq, tq, tk):
    heads, t, dk = qcat.shape
    v_dim, c_dim = w_uv_t.shape[1], w_uv_t.shape[2]
    assert tk % tq == 0 and tq % CHUNK == 0 and seq % tk == 0 and tq % LANES == 0
    nq = seq // tq
    return pl.pallas_call(
        functools.partial(_mla_prompt_kernel, heads=heads, tq=tq, tk=tk),
        grid=(batch, nq),
        in_specs=[
            pl.BlockSpec((heads, tq, dk), lambda b, j: (0, b * nq + j, 0)),
            pl.BlockSpec((seq, dk), lambda b, j: (b, 0)),
            pl.BlockSpec((lat_t.shape[0], seq), lambda b, j: (0, b)),
            pl.BlockSpec(w_uv_t.shape, lambda b, j: (0, 0, 0)),
            pl.BlockSpec(g_out.shape, lambda b, j: (0, 0, 0)),
        ],
        out_specs=pl.BlockSpec((tq, heads * v_dim), lambda b, j: (b * nq + j, 0)),
        out_shape=jax.ShapeDtypeStruct((t, heads * v_dim), BF16),
        scratch_shapes=_mla_scratch(heads * tq, c_dim),
        compiler_params=_params("parallel", "arbitrary"),
        name="mla_prompt",
    )(qcat, kcat, lat_t, w_uv_t, g_out)


def _mla_sample_kernel(q_ref, lat_ref, kr_ref, knew_ref, wuvt_ref, g_ref, o_ref, m_ref, acc_ref,
                       *, heads, tq, tk, c_dim, rope_dim):
    q = q_ref[...].reshape(heads * tq, q_ref.shape[2])
    q_lat = q[:, :c_dim]
    q_rope = q[:, c_dim:c_dim + rope_dim]
    _mla_init(m_ref, acc_ref)
    n_blk = lat_ref.shape[0] // tk

    def latent(kb):
        return lat_ref[pl.ds(pl.multiple_of(kb * tk, tk), tk), :].astype(BF16)

    def scores(kb):
        kr = kr_ref[pl.ds(pl.multiple_of(kb * tk, tk), tk), :].astype(BF16)
        return _mla_scores([q_lat, q_rope], [latent(kb), kr])

    def past_block(kb, s):
        s_next = scores(kb + 1)
        _mla_update(s, latent(kb), m_ref, acc_ref, None, False)
        return s_next

    s = lax.fori_loop(0, n_blk - 1, past_block, scores(0))
    knew = knew_ref[...]
    s_new = _mla_scores([q], [knew])
    _mla_update(s, latent(n_blk - 1), m_ref, acc_ref, None, False)
    _mla_update(s_new, knew[:, :c_dim], m_ref, acc_ref, None, False)
    _mla_finish(acc_ref, wuvt_ref, g_ref, o_ref, heads, tq)


def _mla_sample(qcat, cache_lat, cache_kr, kcat, w_uv_t, g_out, *, tk):
    heads, t, dk = qcat.shape
    batch, n_past, c_dim = cache_lat.shape
    rope_dim = cache_kr.shape[2]
    tq = qh = t // batch
    v_dim = w_uv_t.shape[1]
    assert n_past % CHUNK == 0 and qh <= CHUNK and n_past % tk == 0 and (heads * tq) % LANES == 0
    return pl.pallas_call(
        functools.partial(_mla_sample_kernel, heads=heads, tq=tq, tk=tk, c_dim=c_dim, rope_dim=rope_dim),
        grid=(batch,),
        in_specs=[
            pl.BlockSpec((heads, qh, dk), lambda b: (0, b, 0)),
            pl.BlockSpec((None, n_past, c_dim), lambda b: (b, 0, 0)),
            pl.BlockSpec((None, n_past, rope_dim), lambda b: (b, 0, 0)),
            pl.BlockSpec((qh, dk), lambda b: (b, 0)),
            pl.BlockSpec(w_uv_t.shape, lambda b: (0, 0, 0)),
            pl.BlockSpec(g_out.shape, lambda b: (0, 0, 0)),
        ],
        out_specs=pl.BlockSpec((qh, heads * v_dim), lambda b: (b, 0)),
        out_shape=jax.ShapeDtypeStruct((t, heads * v_dim), BF16),
        scratch_shapes=_mla_scratch(heads * tq, c_dim),
        compiler_params=_params("parallel"),
        name="mla_sample",
    )(qcat, cache_lat, cache_kr, kcat, w_uv_t, g_out)


def _sb_update(q2, kblk, vblk, triu, carry, acc, before, skip_dead=False):
    z = _dot_nt(kblk, q2)
    sub = triu.shape[0]

    def live():
        sp = jnp.maximum(z, 0.0) + jnp.log(1.0 + jnp.exp(-jnp.abs(z)))
        if before is not None:
            sp = jnp.where(before, sp, 0.0)
        hi = sp.astype(BF16)
        lo = (sp - hi.astype(F32)).astype(BF16)
        cs, run = [], carry
        for r0 in reversed(range(0, z.shape[0], sub)):
            cs.insert(0, _dot(triu, hi[r0:r0 + sub]) + _dot(triu, lo[r0:r0 + sub]) + run)
            run = cs[0][0:1, :]
        c = cs[0] if len(cs) == 1 else jnp.concatenate(cs, axis=0)
        a = jnp.exp(z - c)
        if before is not None:
            a = jnp.where(before, a, 0.0)
        return run, acc + _dot_tn(vblk() if callable(vblk) else vblk, a.astype(BF16))

    if not skip_dead:
        return live()
    return lax.cond(jnp.max(z - carry) > DEAD_EXPONENT, live, lambda: (carry, acc))


def _sb_queries(q, head_dim):
    lane = lax.broadcasted_iota(jnp.int32, (1, q.shape[1]), 1) // head_dim
    return jnp.concatenate([jnp.where(lane == h, q, jnp.zeros_like(q)) for h in range(q.shape[1] // head_dim)], axis=0)


def _sb_before(tk, tq, n, first_query_row=0):
    key = lax.broadcasted_iota(jnp.int32, (tk, n), 0) - first_query_row
    qry = lax.broadcasted_iota(jnp.int32, (tk, n), 1) % tq
    return key < qry


def _sb_finish(acc, g_ref, o_ref, head_dim, tq):
    o_ref[...] = _own_block_rms_t(acc, g_ref[...], head_dim, tq).astype(BF16)


def _sb_prompt_kernel(kmax_ref, q_ref, k_ref, v_ref, tri_ref, g_ref, o_ref, *, tq, head_dim):
    b, p, j = pl.program_id(0), pl.program_id(1), pl.program_id(2)
    n_blk = pl.num_programs(2)
    q2 = _sb_queries(q_ref[...], head_dim)
    width = q_ref.shape[1]
    n = q2.shape[0]
    hpg = width // head_dim
    triu = tri_ref[...]
    zero = (jnp.zeros((1, n), F32), jnp.zeros((width, n), F32))

    def own_block():
        return _sb_update(q2, k_ref[pl.ds(0, tq), :], v_ref[pl.ds(0, tq), :], triu, *zero, _sb_before(tq, tq, n))

    def own_and_previous_block():
        k0 = pl.multiple_of((j - 1) * tq, tq)
        return _sb_update(q2, k_ref[pl.ds(k0, 2 * tq), :], v_ref[pl.ds(k0, 2 * tq), :], triu, *zero,
                          _sb_before(2 * tq, tq, n, first_query_row=tq))

    carry, acc = lax.cond(j > 0, own_and_previous_block, own_block)
    q2f = q2.astype(F32)
    qn = jnp.sqrt(_dot_nt(jnp.ones((SUBLANES, width), BF16), (q2f * q2f).astype(BF16))[0:1, :]) * NORM_MARGIN
    lane_head = lax.broadcasted_iota(jnp.int32, (1, n), 1) // tq

    def alive(kb, carry):
        base = (b * n_blk + jnp.maximum(kb, 0)) * (hpg * pl.num_programs(1)) + p * hpg
        kmax = jnp.zeros((1, n), F32)
        for h in range(hpg):
            kmax = jnp.where(lane_head == h, kmax_ref[base + h], kmax)
        return jnp.logical_and(kb >= 0, jnp.max(qn * kmax - carry) > DEAD_EXPONENT)

    def cond(state):
        return state[1]

    def earlier_block(state):
        kb, _, carry, acc = state
        k0 = pl.multiple_of(kb * tq, tq)
        carry, acc = _sb_update(q2, k_ref[pl.ds(k0, tq), :], v_ref[pl.ds(k0, tq), :], triu, carry, acc, None)
        return kb - 1, alive(kb - 1, carry), carry, acc

    state = lax.while_loop(cond, earlier_block, (j - 2, alive(j - 2, carry), carry, acc))
    _sb_finish(state[3], g_ref, o_ref, head_dim, tq)


def _triu(n):
    r = lax.broadcasted_iota(jnp.int32, (n, n), 0)
    c = lax.broadcasted_iota(jnp.int32, (n, n), 1)
    return (c >= r).astype(BF16)


def _sb_prompt(q_sb, k_sb, v_sb, ksq, g_col, *, batch, seq, tq, head_dim, gw):
    t, sbw = q_sb.shape
    nq = seq // tq
    n_grp = sbw // gw
    n_heads = sbw // head_dim
    blk_max = jnp.max(ksq[:, :n_heads].reshape(batch, nq, tq, n_heads), axis=2)
    kmax = (jnp.sqrt(lax.cummax(blk_max, axis=1)) * NORM_MARGIN).reshape(-1)
    return pl.pallas_call(
        functools.partial(_sb_prompt_kernel, tq=tq, head_dim=head_dim),
        grid_spec=pltpu.PrefetchScalarGridSpec(
            num_scalar_prefetch=1,
            grid=(batch, n_grp, nq),
            in_specs=[
                pl.BlockSpec((tq, gw), lambda b, p, j, kmax: (b * nq + j, p)),
                pl.BlockSpec((seq, gw), lambda b, p, j, kmax: (b, p)),
                pl.BlockSpec((seq, gw), lambda b, p, j, kmax: (b, p)),
                pl.BlockSpec((tq, tq), lambda b, p, j, kmax: (0, 0)),
                pl.BlockSpec((gw, 1), lambda b, p, j, kmax: (p, 0)),
            ],
            out_specs=pl.BlockSpec((tq, gw), lambda b, p, j, kmax: (b * nq + j, p)),
        ),
        out_shape=jax.ShapeDtypeStruct((t, sbw), BF16),
        compiler_params=_params("parallel", "parallel", "arbitrary"),
        name="sb_prompt",
    )(kmax, q_sb, k_sb, v_sb, _triu(tq), g_col)


def _sb_sample_kernel(q_ref, kc_ref, vc_ref, kn_ref, vn_ref, tri_ref, g_ref, o_ref, *, tk, head_dim):
    tq, width = q_ref.shape
    q2 = _sb_queries(q_ref[...], head_dim)
    n = q2.shape[0]
    r = lax.broadcasted_iota(jnp.int32, (tq, tq), 0)
    c = lax.broadcasted_iota(jnp.int32, (tq, tq), 1)
    state = _sb_update(q2, kn_ref[...], vn_ref[...], (c >= r).astype(BF16),
                       jnp.zeros((1, n), F32), jnp.zeros((width, n), F32), _sb_before(tq, tq, n))
    triu = tri_ref[...]
    n_blk = kc_ref.shape[0] // tk

    def cached_block(kb, state, skip_dead):
        k0 = pl.multiple_of(kb * tk, tk)
        return _sb_update(q2, kc_ref[pl.ds(k0, tk), :].astype(BF16), lambda: vc_ref[pl.ds(k0, tk), :].astype(BF16),
                          triu, state[0], state[1], None, skip_dead=skip_dead)

    state = cached_block(n_blk - 1, state, False)
    rest = (n_blk - 1) * tk
    z_rest = _dot_nt(kc_ref[:rest, :].astype(BF16), q2)

    def walk():
        return lax.fori_loop(0, n_blk - 1, lambda i, st: cached_block(n_blk - 2 - i, st, True), state)

    state = lax.cond(jnp.max(z_rest - state[0]) > DEAD_EXPONENT, walk, lambda: state)
    _sb_finish(state[1], g_ref, o_ref, head_dim, tq)


def _sb_sample(q_sb, cache_k, cache_v, k_sb, v_sb, g_col, *, tk, head_dim):
    t, sbw = q_sb.shape
    batch, n_past, _ = cache_k.shape
    tq = t // batch
    n_grp = sbw // LANES
    assert n_past % tk == 0
    new = pl.BlockSpec((tq, LANES), lambda b, p: (b, p))
    past = pl.BlockSpec((None, n_past, LANES), lambda b, p: (b, 0, p))
    return pl.pallas_call(
        functools.partial(_sb_sample_kernel, tk=tk, head_dim=head_dim),
        grid=(batch, n_grp),
        in_specs=[new, past, past, new, new,
                  pl.BlockSpec((tk, tk), lambda b, p: (0, 0)),
                  pl.BlockSpec((LANES, 1), lambda b, p: (p, 0))],
        out_specs=new,
        out_shape=jax.ShapeDtypeStruct((t, sbw), BF16),
        compiler_params=_params("parallel", "parallel"),
        name="sb_sample",
    )(q_sb, cache_k, cache_v, k_sb, v_sb, _triu(tk), g_col)


def _out_proj_kernel(oa_ref, ob_ref, h_ref, wa_ref, wb_ref, g_ref, o_ref):
    m = _dot(oa_ref[...], wa_ref[...]) + _dot(ob_ref[...], wb_ref[...])
    o_ref[...] = h_ref[...] + _rms(m, g_ref[...])


def _out_proj(o_a, o_b, h, w_a, w_b, g_post, *, tm):
    t, d = h.shape
    row = lambda i: (i, 0)
    const = lambda i: (0, 0)
    return pl.pallas_call(
        _out_proj_kernel,
        grid=(t // tm,),
        in_specs=[
            pl.BlockSpec((tm, o_a.shape[1]), row),
            pl.BlockSpec((tm, o_b.shape[1]), row),
            pl.BlockSpec((tm, d), row),
            pl.BlockSpec(w_a.shape, const),
            pl.BlockSpec(w_b.shape, const),
            pl.BlockSpec((1, d), const),
        ],
        out_specs=pl.BlockSpec((tm, d), row),
        out_shape=jax.ShapeDtypeStruct((t, d), F32),
        compiler_params=_params("parallel"),
        name="out_proj",
    )(o_a, o_b, h, w_a, w_b, g_post)


def _rot_cols(w):
    half = w.shape[-1] // 2
    return jnp.concatenate([-w[..., half:], w[..., :half]], axis=-1)


def _rope_tables(pos, rope_dim, rows):
    half = rope_dim // 2
    inv_freq = ROPE_THETA ** (-jnp.arange(half, dtype=F32) / half)
    ang = pos.astype(F32)[:, None] * inv_freq[None, :]
    pad = jnp.zeros((pos.shape[0], LANES - rope_dim), F32)
    cosp = jnp.concatenate([jnp.cos(ang), jnp.cos(ang), pad], axis=1)
    sinp = jnp.concatenate([jnp.sin(ang), jnp.sin(ang), pad], axis=1)
    reps = max(1, rows // pos.shape[0])
    return jnp.tile(cosp, (reps, 1)), jnp.tile(sinp, (reps, 1))


def kernel(x_prompt, x_sample, cache_mla_latent, cache_mla_krope, cache_sb_k, cache_sb_v, g_pre_ff1, w_gate1, w_up1, w_down1, g_post_ff1, g_pre_mix, w_in, g_q, w_uq, g_kv, w_uk, w_uv, g_mla_out, g_sb_out, w_out, g_post_mix, g_pre_ff2, w_gate2, w_up2, w_down2, g_post_ff2, g_final):
    depth = w_in.shape[0]
    assert depth == 1
    batch, seq, d = x_prompt.shape
    dec_batch, dec_seq, _ = x_sample.shape
    n_past = cache_mla_latent.shape[2]
    _, q_lora, heads, qk_dim = w_uq.shape
    kv_lora, _, nope = w_uk.shape[1:]
    rope_dim = qk_dim - nope
    v_dim = w_uv.shape[3]
    sb_heads, sb_dim = g_sb_out.shape[1:]
    sbw = sb_heads * sb_dim
    mla_scale = float(qk_dim) ** -0.5
    sb_scale = float(sb_dim) ** -0.5
    assert kv_lora == LANES and 2 * rope_dim <= LANES and LANES % sb_dim == 0
    dims = (heads, q_lora, kv_lora, rope_dim, sbw, sb_heads, mla_scale * LOG2E, sb_scale)

    row = lambda g: g.reshape(1, -1).astype(F32)
    l = 0
    wi = w_in[l]
    i1 = q_lora + kv_lora
    i2 = i1 + rope_dim
    w_kr = wi[:, i1:i2]
    w_in_ext = jnp.concatenate(
        [wi[:, :i1], w_kr, _rot_cols(w_kr), jnp.zeros((d, LANES - 2 * rope_dim), F32), wi[:, i2:]], axis=1).astype(BF16)
    wq = w_uq[l]
    wq_rope = wq[:, :, nope:]
    wq_grp = jnp.concatenate(
        [wq_rope, _rot_cols(wq_rope), jnp.zeros((q_lora, heads, LANES - 2 * rope_dim), F32)], axis=2)
    w_uq_ext = jnp.concatenate(
        [wq[:, :, :nope].reshape(q_lora, heads * nope), wq_grp.reshape(q_lora, heads * LANES)], axis=1).astype(BF16)
    wk = jnp.transpose(w_uk[l], (1, 2, 0))
    eye = jnp.eye(heads, dtype=F32)
    w_uk_bd = (wk[:, :, None, :] * eye[:, None, :, None]).reshape(heads * nope, heads * kv_lora).astype(BF16)
    w_uv_t = jnp.transpose(w_uv[l], (1, 2, 0)).astype(BF16)
    w_out_a = w_out[l][:heads * v_dim].astype(BF16)
    w_out_b = w_out[l][heads * v_dim:].astype(BF16)
    ffn1 = (row(g_pre_ff1[l]), w_gate1[l].astype(BF16), w_up1[l].astype(BF16), w_down1[l].astype(BF16),
            row(g_post_ff1[l]), row(g_final[l]))
    ffn2 = (row(g_pre_ff2[l]), w_gate2[l].astype(BF16), w_up2[l].astype(BF16), w_down2[l].astype(BF16),
            row(g_post_ff2[l]), row(g_final[l]))
    g_mla = g_mla_out[l].reshape(heads, v_dim, 1).astype(F32)
    g_sb = g_sb_out[l].reshape(sbw, 1).astype(F32)

    def layer(x, pos, tm, attend):
        h = _ffn(x, *ffn1, final=False, tm=tm)
        cosp, sinp = _rope_tables(pos, rope_dim, tm)
        lat, kr, kn, vn, qcat, kcat, lat_t, q_sb, k_sb, v_sb, ksq = _mix_in(
            h, row(g_pre_mix[l]), w_in_ext, row(g_q[l]), w_uq_ext, w_uk_bd, row(g_kv[l]), cosp, sinp,
            tm=tm, dims=dims)
        o_mla, o_sb = attend(qcat, kcat, lat_t, q_sb, k_sb, v_sb, ksq)
        h = _out_proj(o_mla, o_sb, h, w_out_a, w_out_b, row(g_post_mix[l]), tm=tm)
        y = _ffn(h, *ffn2, final=True, tm=tm)
        return y, (lat, kr, kn, vn)

    def attend_prompt(qcat, kcat, lat_t, q_sb, k_sb, v_sb, ksq):
        o_mla = _mla_prompt(qcat, kcat, lat_t, w_uv_t, g_mla, batch=batch, seq=seq, tq=512, tk=512)
        o_sb = _sb_prompt(q_sb, k_sb, v_sb, ksq, g_sb, batch=batch, seq=seq, tq=256, head_dim=sb_dim, gw=2 * LANES)
        return o_mla, o_sb

    def attend_sample(qcat, kcat, lat_t, q_sb, k_sb, v_sb, ksq):
        o_mla = _mla_sample(qcat, cache_mla_latent[l], cache_mla_krope[l], kcat, w_uv_t, g_mla, tk=512)
        o_sb = _sb_sample(q_sb, cache_sb_k[l].reshape(dec_batch, n_past, sbw),
                          cache_sb_v[l].reshape(dec_batch, n_past, sbw), k_sb, v_sb, g_sb, tk=256, head_dim=sb_dim)
        return o_mla, o_sb

    pos_p = jnp.arange(seq, dtype=jnp.int32)
    pos_s = n_past + jnp.arange(dec_seq, dtype=jnp.int32)
    yp, rp = layer(x_prompt.reshape(batch * seq, d), pos_p, 512, attend_prompt)
    ys, rs = layer(x_sample.reshape(dec_batch * dec_seq, d), pos_s, 512, attend_sample)

    def rows(r, b, s):
        lat, kr, kn, vn = r
        return (lat.reshape(1, b, s, kv_lora), kr.reshape(1, b, s, rope_dim),
                kn.reshape(1, b, s, sb_heads, sb_dim), vn.reshape(1, b, s, sb_heads, sb_dim))

    return (yp.reshape(batch, seq, d), ys.reshape(dec_batch, dec_seq, d)) + rows(rp, batch, seq) + rows(rs, dec_batch, dec_seq)
```

```python
import functools

import jax
import jax.numpy as jnp
from jax import lax
from jax.experimental import pallas as pl
from jax.experimental.pallas import tpu as pltpu

EPS = 1e-6
CHUNK = 64
ROPE_THETA = 10000.0
LANES = 128
F32 = jnp.float32
BF16 = jnp.bfloat16
VMEM_LIMIT = 56 * 1024 * 1024
SUBLANES = 8
MXU_WIDTH = 256
ONES_ROWS = 2 * SUBLANES
LOG2E = 1.4426950408889634
DEAD_EXPONENT = -120.0
NORM_MARGIN = 1.02
MLA_SKEW = (2, 3)
MLA_GROUP_LANES = 512


def _dot(a, b):
    return jnp.dot(a, b, preferred_element_type=F32)


def _dot_nt(a, b):
    return lax.dot_general(a, b, (((1,), (1,)), ((), ())), preferred_element_type=F32)


def _dot_tn(a, b):
    return lax.dot_general(a, b, (((0,), (0,)), ((), ())), preferred_element_type=F32)


def _rms(x, g):
    return x * lax.rsqrt(jnp.mean(x * x, axis=-1, keepdims=True) + EPS) * g


def _params(*sem):
    return pltpu.CompilerParams(dimension_semantics=sem, vmem_limit_bytes=VMEM_LIMIT)


def _ffn_kernel(x_ref, gpre_ref, wg_ref, wu_ref, wd_ref, gpost_ref, gfin_ref, o_ref, *, final, fc):
    x = x_ref[...]
    xn = _rms(x, gpre_ref[...]).astype(BF16)
    d_ff = wg_ref.shape[1]
    acc = jnp.zeros(x.shape, F32)
    for c in range(0, d_ff, fc):
        g = _dot(xn, wg_ref[:, c:c + fc])
        u = _dot(xn, wu_ref[:, c:c + fc])
        a = (g * jax.nn.sigmoid(g)) * u
        acc = acc + _dot(a.astype(BF16), wd_ref[c:c + fc, :])
    h = x + 0.5 * _rms(acc, gpost_ref[...])
    if final:
        h = _rms(h, gfin_ref[...])
    o_ref[...] = h


def _ffn(x, g_pre, wg, wu, wd, g_post, g_fin, *, final, tm):
    t, d = x.shape
    d_ff = wg.shape[1]
    fc = MXU_WIDTH
    assert d_ff % fc == 0
    const = lambda i: (0, 0)
    return pl.pallas_call(
        functools.partial(_ffn_kernel, final=final, fc=fc),
        grid=(t // tm,),
        in_specs=[
            pl.BlockSpec((tm, d), lambda i: (i, 0)),
            pl.BlockSpec((1, d), const),
            pl.BlockSpec((d, d_ff), const, pipeline_mode=pl.Buffered(1)),
            pl.BlockSpec((d, d_ff), const, pipeline_mode=pl.Buffered(1)),
            pl.BlockSpec((d_ff, d), const, pipeline_mode=pl.Buffered(1)),
            pl.BlockSpec((1, d), const),
            pl.BlockSpec((1, d), const),
        ],
        out_specs=pl.BlockSpec((tm, d), lambda i: (i, 0)),
        out_shape=jax.ShapeDtypeStruct((t, d), F32),
        compiler_params=_params("parallel"),
        name="ffn_final" if final else "ffn",
    )(x, g_pre, wg, wu, wd, g_post, g_fin)


def _mix_in_kernel(h_ref, gpre_ref, win_ref, gq_ref, wuq_ref, wuk_ref, gkv_ref, cos_ref, sin_ref, sel_ref,
                   lat_ref, kr_ref, kn_ref, vn_ref, qcat_ref, kcat_ref, latt_ref, qsb_ref, ksb_ref, vsb_ref, ksq_ref,
                   *, heads, q_lora, kv_lora, rope_dim, sbw, mla_scale, sb_scale):
    u = _rms(h_ref[...], gpre_ref[...]).astype(BF16)
    proj = _dot(u, win_ref[...])
    cosp = cos_ref[...]
    sinp = sin_ref[...]
    o1 = q_lora
    o2 = o1 + kv_lora
    o3 = o2 + LANES

    def rope(grp):
        return grp * cosp + pltpu.roll(grp, LANES - rope_dim, 1) * sinp

    latent = _rms(proj[:, o1:o2], gkv_ref[...])
    krope = rope(proj[:, o2:o3])
    lat_ref[...] = latent
    kr_ref[...] = krope[:, :rope_dim]
    k_sb = proj[:, o3 + sbw:o3 + 2 * sbw]
    v_sb = proj[:, o3 + 2 * sbw:o3 + 3 * sbw]
    kn_ref[...] = k_sb
    vn_ref[...] = v_sb
    k_bf = k_sb.astype(BF16)
    ksb_ref[...] = k_bf
    vsb_ref[...] = v_sb.astype(BF16)
    qsb_ref[...] = (proj[:, o3:o3 + sbw] * sb_scale).astype(BF16)
    kcat_ref[...] = jnp.concatenate([latent, krope], axis=1).astype(BF16)
    latt_ref[...] = jnp.concatenate([latent.T, jnp.ones((ONES_ROWS, latent.shape[0]), F32)], axis=0).astype(BF16)
    k_f = k_bf.astype(F32)
    ksq_ref[...] = _dot((k_f * k_f).astype(BF16), sel_ref[...])

    c_q = _rms(proj[:, :o1], gq_ref[...]).astype(BF16)
    q = _dot(c_q, wuq_ref[...])
    n_nope = wuk_ref.shape[0]
    q_lat = _dot(q[:, :n_nope].astype(BF16), wuk_ref[...])
    for hd in range(heads):
        ql = q_lat[:, hd * kv_lora:(hd + 1) * kv_lora] * mla_scale
        qr = rope(q[:, n_nope + hd * LANES:n_nope + (hd + 1) * LANES]) * mla_scale
        qcat_ref[hd] = jnp.concatenate([ql, qr], axis=1).astype(BF16)


def _mix_in(h, g_pre, w_in_ext, g_q, w_uq_ext, w_uk_bd, g_kv, cosp, sinp, *, tm, dims):
    t, d = h.shape
    heads, q_lora, kv_lora, rope_dim, sbw, sb_heads, mla_scale, sb_scale = dims
    n_tab = cosp.shape[0] // tm
    const = lambda i: (0, 0)
    row = lambda i: (i, 0)
    tab = lambda i: (i % n_tab, 0)
    out_shape = (
        jax.ShapeDtypeStruct((t, kv_lora), F32),
        jax.ShapeDtypeStruct((t, rope_dim), F32),
        jax.ShapeDtypeStruct((t, sbw), F32),
        jax.ShapeDtypeStruct((t, sbw), F32),
        jax.ShapeDtypeStruct((heads, t, 2 * LANES), BF16),
        jax.ShapeDtypeStruct((t, 2 * LANES), BF16),
        jax.ShapeDtypeStruct((kv_lora + ONES_ROWS, t), BF16),
        jax.ShapeDtypeStruct((t, sbw), BF16),
        jax.ShapeDtypeStruct((t, sbw), BF16),
        jax.ShapeDtypeStruct((t, sbw), BF16),
        jax.ShapeDtypeStruct((t, LANES), F32),
    )
    out_specs = (
        pl.BlockSpec((tm, kv_lora), row),
        pl.BlockSpec((tm, rope_dim), row),
        pl.BlockSpec((tm, sbw), row),
        pl.BlockSpec((tm, sbw), row),
        pl.BlockSpec((heads, tm, 2 * LANES), lambda i: (0, i, 0)),
        pl.BlockSpec((tm, 2 * LANES), row),
        pl.BlockSpec((kv_lora + ONES_ROWS, tm), lambda i: (0, i)),
        pl.BlockSpec((tm, sbw), row),
        pl.BlockSpec((tm, sbw), row),
        pl.BlockSpec((tm, sbw), row),
        pl.BlockSpec((tm, LANES), row),
    )
    sel = (lax.broadcasted_iota(jnp.int32, (sbw, LANES), 0) // (sbw // sb_heads)
           == lax.broadcasted_iota(jnp.int32, (sbw, LANES), 1)).astype(BF16)
    return pl.pallas_call(
        functools.partial(_mix_in_kernel, heads=heads, q_lora=q_lora, kv_lora=kv_lora,
                          rope_dim=rope_dim, sbw=sbw, mla_scale=mla_scale, sb_scale=sb_scale),
        grid=(t // tm,),
        in_specs=[
            pl.BlockSpec((tm, d), row),
            pl.BlockSpec((1, d), const),
            pl.BlockSpec(w_in_ext.shape, const),
            pl.BlockSpec((1, q_lora), const),
            pl.BlockSpec(w_uq_ext.shape, const),
            pl.BlockSpec(w_uk_bd.shape, const),
            pl.BlockSpec((1, kv_lora), const),
            pl.BlockSpec((tm, LANES), tab),
            pl.BlockSpec((tm, LANES), tab),
            pl.BlockSpec((sbw, LANES), const),
        ],
        out_specs=out_specs,
        out_shape=out_shape,
        compiler_params=_params("parallel"),
        name="mix_in",
    )(h, g_pre, w_in_ext, g_q, w_uq_ext, w_uk_bd, g_kv, cosp, sinp, sel)


def _mla_scores(qs, ks):
    return sum(_dot_nt(kk, qq) for qq, kk in zip(qs, ks))


def _mla_probs(s, m_ref, visible):
    if visible is not None:
        s = jnp.where(visible, s, -jnp.inf)
    m_prev = m_ref[...]
    m_new = jnp.maximum(m_prev, jnp.max(s, axis=0, keepdims=True))
    m_ref[...] = m_new
    return jnp.exp2(m_prev - m_new), jnp.exp2(s - m_new)


def _mla_accumulate(alpha, p, v, acc_ref, v_transposed):
    pb = p.astype(BF16)
    if v_transposed:
        pv = _dot(v, pb)
    else:
        l = jnp.sum(p, axis=0, keepdims=True)
        pv = jnp.concatenate([_dot_tn(v, pb), jnp.broadcast_to(l, (ONES_ROWS, l.shape[1]))], axis=0)
    acc_ref[...] = alpha * acc_ref[...] + pv


def _mla_update(s, v, m_ref, acc_ref, visible, v_transposed):
    alpha, p = _mla_probs(s, m_ref, visible)
    _mla_accumulate(alpha, p, v, acc_ref, v_transposed)


def _own_block_rms_t(x, g_col, head_rows, tq):
    r = lax.broadcasted_iota(jnp.int32, x.shape, 0) // head_rows
    c = lax.broadcasted_iota(jnp.int32, x.shape, 1) // tq
    o = jnp.where(r == c, x, 0.0)
    inv = lax.rsqrt(jnp.sum(o * o, axis=0, keepdims=True) / head_rows + EPS)
    o_t = (o * inv * g_col).T
    out = o_t[:tq]
    for h in range(1, x.shape[1] // tq):
        out = out + o_t[h * tq:(h + 1) * tq]
    return out


def _mla_finish(acc_ref, wuvt_ref, g_ref, o_ref, heads, tq):
    c_dim = acc_ref.shape[0] - ONES_ROWS
    v_dim = wuvt_ref.shape[1]
    o_lat = (acc_ref[:c_dim, :] / acc_ref[c_dim:c_dim + 1, :]).astype(BF16)
    if tq % LANES == 0:
        outs = []
        for hd in range(heads):
            o = _dot(wuvt_ref[hd], o_lat[:, hd * tq:(hd + 1) * tq])
            outs.append(o * lax.rsqrt(jnp.mean(o * o, axis=0, keepdims=True) + EPS) * g_ref[hd])
        o_t = jnp.concatenate(outs, axis=0).T
    else:
        full = _dot(wuvt_ref[...].reshape(heads * v_dim, c_dim), o_lat)
        o_t = _own_block_rms_t(full, g_ref[...].reshape(heads * v_dim, 1), v_dim, tq)
    o_ref[...] = o_t.astype(BF16)


def _mla_init(m_ref, acc_ref):
    m_ref[...] = jnp.full(m_ref.shape, -jnp.inf, F32)
    acc_ref[...] = jnp.zeros(acc_ref.shape, F32)


def _mla_prompt_kernel(q_ref, k_ref, vt_ref, wuvt_ref, g_ref, o_ref, m_ref, acc_ref, *, heads, tq, tk):
    j = pl.program_id(1)
    q0 = j * tq
    q = q_ref[...].reshape(heads * tq, q_ref.shape[2])
    _mla_init(m_ref, acc_ref)
    n_full = q0 // tk

    def blocks(todo):
        lanes = [slice(r0, r0 + MLA_GROUP_LANES) for r0 in range(0, heads * tq, MLA_GROUP_LANES)]
        units = [(kb, vis, ln) for kb, vis in todo for ln in lanes]

        def rows(kb):
            return pl.ds(pl.multiple_of(kb * tk, tk), tk)

        s, ap = {}, {}
        lag_p, lag_a = MLA_SKEW
        for step in range(len(units) + lag_a):
            if step < len(units):
                kb, _, ln = units[step]
                s[step] = _dot_nt(k_ref[rows(kb), :], q[ln])
            u = step - lag_p
            if 0 <= u < len(units):
                kb, vis, ln = units[u]
                ap[u] = _mla_probs(s.pop(u), m_ref.at[:, ln], None if vis is None else vis[:, ln])
            u = step - lag_a
            if 0 <= u < len(units):
                kb, _, ln = units[u]
                _mla_accumulate(*ap.pop(u), vt_ref[:, rows(kb)], acc_ref.at[:, ln], True)

    def full_block(kb, carry):
        blocks([(kb, None)])
        return carry

    lax.fori_loop(0, n_full, full_block, 0)
    kpos = n_full * tk + lax.broadcasted_iota(jnp.int32, (tk, heads * tq), 0)
    qpos = q0 + lax.broadcasted_iota(jnp.int32, (tk, heads * tq), 1) % tq
    blocks([(n_full, (kpos // CHUNK) <= (qpos // CHUNK))])
    _mla_finish(acc_ref, wuvt_ref, g_ref, o_ref, heads, tq)


def _mla_scratch(rows, c_dim):
    return [pltpu.VMEM((1, rows), F32), pltpu.VMEM((c_dim + ONES_ROWS, rows), F32)]


def _mla_prompt(qcat, kcat, lat_t, w_uv_t, g_out, *, batch, seq, tq, tk):
    heads, t, dk = qcat.shape
    v_dim, c_dim = w_uv_t.shape[1], w_uv_t.shape[2]
    assert tk % tq == 0 and tq % CHUNK == 0 and seq % tk == 0 and tq % LANES == 0
    nq = seq // tq
    return pl.pallas_call(
        functools.partial(_mla_prompt_kernel, heads=heads, tq=tq, tk=tk),
        grid=(batch, nq),
        in_specs=[
            pl.BlockSpec((heads, tq, dk), lambda b, j: (0, b * nq + j, 0)),
            pl.BlockSpec((seq, dk), lambda b, j: (b, 0)),
            pl.BlockSpec((lat_t.shape[0], seq), lambda b, j: (0, b)),
            pl.BlockSpec(w_uv_t.shape, lambda b, j: (0, 0, 0)),
            pl.BlockSpec(g_out.shape, lambda b, j: (0, 0, 0)),
        ],
        out_specs=pl.BlockSpec((tq, heads * v_dim), lambda b, j: (b * nq + j, 0)),
        out_shape=jax.ShapeDtypeStruct((t, heads * v_dim), BF16),
        scratch_shapes=_mla_scratch(heads * tq, c_dim),
        compiler_params=_params("parallel", "arbitrary"),
        name="mla_prompt",
    )(qcat, kcat, lat_t, w_uv_t, g_out)


def _mla_sample_kernel(q_ref, lat_ref, kr_ref, knew_ref, wuvt_ref, g_ref, o_ref, m_ref, acc_ref,
                       *, heads, tq, tk, c_dim, rope_dim):
    q = q_ref[...].reshape(heads * tq, q_ref.shape[2])
    q_lat = q[:, :c_dim]
    q_rope = q[:, c_dim:c_dim + rope_dim]
    _mla_init(m_ref, acc_ref)
    n_blk = lat_ref.shape[0] // tk

    def latent(kb):
        return lat_ref[pl.ds(pl.multiple_of(kb * tk, tk), tk), :].astype(BF16)

    def scores(kb):
        kr = kr_ref[pl.ds(pl.multiple_of(kb * tk, tk), tk), :].astype(BF16)
        return _mla_scores([q_lat, q_rope], [latent(kb), kr])

    def past_block(kb, s):
        s_next = scores(kb + 1)
        _mla_update(s, latent(kb), m_ref, acc_ref, None, False)
        return s_next

    s = lax.fori_loop(0, n_blk - 1, past_block, scores(0))
    knew = knew_ref[...]
    s_new = _mla_scores([q], [knew])
    _mla_update(s, latent(n_blk - 1), m_ref, acc_ref, None, False)
    _mla_update(s_new, knew[:, :c_dim], m_ref, acc_ref, None, False)
    _mla_finish(acc_ref, wuvt_ref, g_ref, o_ref, heads, tq)


def _mla_sample(qcat, cache_lat, cache_kr, kcat, w_uv_t, g_out, *, tk):
    heads, t, dk = qcat.shape
    batch, n_past, c_dim = cache_lat.shape
    rope_dim = cache_kr.shape[2]
    tq = qh = t // batch
    v_dim = w_uv_t.shape[1]
    assert n_past % CHUNK == 0 and qh <= CHUNK and n_past % tk == 0 and (heads * tq) % LANES == 0
    return pl.pallas_call(
        functools.partial(_mla_sample_kernel, heads=heads, tq=tq, tk=tk, c_dim=c_dim, rope_dim=rope_dim),
        grid=(batch,),
        in_specs=[
            pl.BlockSpec((heads, qh, dk), lambda b: (0, b, 0)),
            pl.BlockSpec((None, n_past, c_dim), lambda b: (b, 0, 0)),
            pl.BlockSpec((None, n_past, rope_dim), lambda b: (b, 0, 0)),
            pl.BlockSpec((qh, dk), lambda b: (b, 0)),
            pl.BlockSpec(w_uv_t.shape, lambda b: (0, 0, 0)),
            pl.BlockSpec(g_out.shape, lambda b: (0, 0, 0)),
        ],
        out_specs=pl.BlockSpec((qh, heads * v_dim), lambda b: (b, 0)),
        out_shape=jax.ShapeDtypeStruct((t, heads * v_dim), BF16),
        scratch_shapes=_mla_scratch(heads * tq, c_dim),
        compiler_params=_params("parallel"),
        name="mla_sample",
    )(qcat, cache_lat, cache_kr, kcat, w_uv_t, g_out)


def _sb_update(q2, kblk, vblk, triu, carry, acc, before, skip_dead=False):
    z = _dot_nt(kblk, q2)
    sub = triu.shape[0]

    def live():
        sp = jnp.maximum(z, 0.0) + jnp.log(1.0 + jnp.exp(-jnp.abs(z)))
        if before is not None:
            sp = jnp.where(before, sp, 0.0)
        hi = sp.astype(BF16)
        lo = (sp - hi.astype(F32)).astype(BF16)
        cs, run = [], carry
        for r0 in reversed(range(0, z.shape[0], sub)):
            cs.insert(0, _dot(triu, hi[r0:r0 + sub]) + _dot(triu, lo[r0:r0 + sub]) + run)
            run = cs[0][0:1, :]
        c = cs[0] if len(cs) == 1 else jnp.concatenate(cs, axis=0)
        a = jnp.exp(z - c)
        if before is not None:
            a = jnp.where(before, a, 0.0)
        return run, acc + _dot_tn(vblk() if callable(vblk) else vblk, a.astype(BF16))

    if not skip_dead:
        return live()
    return lax.cond(jnp.max(z - carry) > DEAD_EXPONENT, live, lambda: (carry, acc))


def _sb_queries(q, head_dim):
    lane = lax.broadcasted_iota(jnp.int32, (1, q.shape[1]), 1) // head_dim
    return jnp.concatenate([jnp.where(lane == h, q, jnp.zeros_like(q)) for h in range(q.shape[1] // head_dim)], axis=0)


def _sb_query_norms(q2):
    q2f = q2.astype(F32)
    return jnp.sqrt(_dot_nt(jnp.ones((SUBLANES, q2.shape[1]), BF16), (q2f * q2f).astype(BF16))[0:1, :]) * NORM_MARGIN


def _sb_before(tk, tq, n, first_query_row=0):
    key = lax.broadcasted_iota(jnp.int32, (tk, n), 0) - first_query_row
    qry = lax.broadcasted_iota(jnp.int32, (tk, n), 1) % tq
    return key < qry


def _sb_finish(acc, g_ref, o_ref, head_dim, tq):
    o_ref[...] = _own_block_rms_t(acc, g_ref[...], head_dim, tq).astype(BF16)


def _sb_prompt_kernel(kmax_ref, q_ref, k_ref, v_ref, tri_ref, g_ref, o_ref, *, tq, head_dim):
    b, p, j = pl.program_id(0), pl.program_id(1), pl.program_id(2)
    n_blk = pl.num_programs(2)
    q2 = _sb_queries(q_ref[...], head_dim)
    width = q_ref.shape[1]
    n = q2.shape[0]
    hpg = width // head_dim
    triu = tri_ref[...]
    zero = (jnp.zeros((1, n), F32), jnp.zeros((width, n), F32))

    def own_block():
        return _sb_update(q2, k_ref[pl.ds(0, tq), :], v_ref[pl.ds(0, tq), :], triu, *zero, _sb_before(tq, tq, n))

    def own_and_previous_block():
        k0 = pl.multiple_of((j - 1) * tq, tq)
        return _sb_update(q2, k_ref[pl.ds(k0, 2 * tq), :], v_ref[pl.ds(k0, 2 * tq), :], triu, *zero,
                          _sb_before(2 * tq, tq, n, first_query_row=tq))

    carry, acc = lax.cond(j > 0, own_and_previous_block, own_block)
    qn = _sb_query_norms(q2)
    lane_head = lax.broadcasted_iota(jnp.int32, (1, n), 1) // tq

    def alive(kb, carry):
        base = (b * n_blk + jnp.maximum(kb, 0)) * (hpg * pl.num_programs(1)) + p * hpg
        kmax = jnp.zeros((1, n), F32)
        for h in range(hpg):
            kmax = jnp.where(lane_head == h, kmax_ref[base + h], kmax)
        return jnp.logical_and(kb >= 0, jnp.max(qn * kmax - carry) > DEAD_EXPONENT)

    def cond(state):
        return state[1]

    def earlier_block(state):
        kb, _, carry, acc = state
        k0 = pl.multiple_of(kb * tq, tq)
        carry, acc = _sb_update(q2, k_ref[pl.ds(k0, tq), :], v_ref[pl.ds(k0, tq), :], triu, carry, acc, None)
        return kb - 1, alive(kb - 1, carry), carry, acc

    state = lax.while_loop(cond, earlier_block, (j - 2, alive(j - 2, carry), carry, acc))
    _sb_finish(state[3], g_ref, o_ref, head_dim, tq)


def _triu(n):
    r = lax.broadcasted_iota(jnp.int32, (n, n), 0)
    c = lax.broadcasted_iota(jnp.int32, (n, n), 1)
    return (c >= r).astype(BF16)


def _sb_prompt(q_sb, k_sb, v_sb, ksq, g_col, *, batch, seq, tq, head_dim, gw):
    t, sbw = q_sb.shape
    nq = seq // tq
    n_grp = sbw // gw
    n_heads = sbw // head_dim
    blk_max = jnp.max(ksq[:, :n_heads].reshape(batch, nq, tq, n_heads), axis=2)
    kmax = (jnp.sqrt(lax.cummax(blk_max, axis=1)) * NORM_MARGIN).reshape(-1)
    return pl.pallas_call(
        functools.partial(_sb_prompt_kernel, tq=tq, head_dim=head_dim),
        grid_spec=pltpu.PrefetchScalarGridSpec(
            num_scalar_prefetch=1,
            grid=(batch, n_grp, nq),
            in_specs=[
                pl.BlockSpec((tq, gw), lambda b, p, j, kmax: (b * nq + j, p)),
                pl.BlockSpec((seq, gw), lambda b, p, j, kmax: (b, p)),
                pl.BlockSpec((seq, gw), lambda b, p, j, kmax: (b, p)),
                pl.BlockSpec((tq, tq), lambda b, p, j, kmax: (0, 0)),
                pl.BlockSpec((gw, 1), lambda b, p, j, kmax: (p, 0)),
            ],
            out_specs=pl.BlockSpec((tq, gw), lambda b, p, j, kmax: (b * nq + j, p)),
        ),
        out_shape=jax.ShapeDtypeStruct((t, sbw), BF16),
        compiler_params=_params("parallel", "parallel", "arbitrary"),
        name="sb_prompt",
    )(kmax, q_sb, k_sb, v_sb, _triu(tq), g_col)


def _sb_sample_kernel(q_ref, kc_ref, vc_ref, kn_ref, vn_ref, tri_ref, g_ref, o_ref, *, tk, head_dim):
    tq, width = q_ref.shape
    q2 = _sb_queries(q_ref[...], head_dim)
    n = q2.shape[0]
    r = lax.broadcasted_iota(jnp.int32, (tq, tq), 0)
    c = lax.broadcasted_iota(jnp.int32, (tq, tq), 1)
    state = _sb_update(q2, kn_ref[...], vn_ref[...], (c >= r).astype(BF16),
                       jnp.zeros((1, n), F32), jnp.zeros((width, n), F32), _sb_before(tq, tq, n))
    triu = tri_ref[...]
    n_blk = kc_ref.shape[0] // tk

    def cached_block(kb, state, skip_dead):
        k0 = pl.multiple_of(kb * tk, tk)
        return _sb_update(q2, kc_ref[pl.ds(k0, tk), :].astype(BF16), lambda: vc_ref[pl.ds(k0, tk), :].astype(BF16),
                          triu, state[0], state[1], None, skip_dead=skip_dead)

    state = cached_block(n_blk - 1, state, False)
    rest = (n_blk - 1) * tk
    z_rest = _dot_nt(kc_ref[:rest, :].astype(BF16), q2)

    def walk():
        return lax.fori_loop(0, n_blk - 1, lambda i, st: cached_block(n_blk - 2 - i, st, True), state)

    state = lax.cond(jnp.max(z_rest - state[0]) > DEAD_EXPONENT, walk, lambda: state)
    _sb_finish(state[1], g_ref, o_ref, head_dim, tq)


def _sb_sample_full(q_sb, cache_k, cache_v, k_sb, v_sb, g_col, *, tk, head_dim):
    t, sbw = q_sb.shape
    batch, n_past, _ = cache_k.shape
    tq = t // batch
    n_grp = sbw // LANES
    assert n_past % tk == 0
    new = pl.BlockSpec((tq, LANES), lambda b, p: (b, p))
    past = pl.BlockSpec((None, n_past, LANES), lambda b, p: (b, 0, p))
    return pl.pallas_call(
        functools.partial(_sb_sample_kernel, tk=tk, head_dim=head_dim),
        grid=(batch, n_grp),
        in_specs=[new, past, past, new, new,
                  pl.BlockSpec((tk, tk), lambda b, p: (0, 0)),
                  pl.BlockSpec((LANES, 1), lambda b, p: (p, 0))],
        out_specs=new,
        out_shape=jax.ShapeDtypeStruct((t, sbw), BF16),
        compiler_params=_params("parallel", "parallel"),
        name="sb_sample_full",
    )(q_sb, cache_k, cache_v, k_sb, v_sb, _triu(tk), g_col)


def _key_norms_kernel(k_ref, o_ref):
    x = k_ref[...]
    blk = jnp.max(jnp.sum(x * x, axis=-1, keepdims=True), axis=0)
    cur = jnp.broadcast_to(blk, o_ref.shape)

    @pl.when(pl.program_id(1) == 0)
    def _():
        o_ref[...] = cur

    @pl.when(pl.program_id(1) != 0)
    def _():
        o_ref[...] = jnp.maximum(o_ref[...], cur)


def _key_norms(cache_k, *, tkn):
    batch, n_past, n_heads, dim = cache_k.shape
    assert n_past % tkn == 0
    return pl.pallas_call(
        _key_norms_kernel,
        grid=(batch, n_past // tkn),
        in_specs=[pl.BlockSpec((None, tkn, n_heads, dim), lambda b, c: (b, c, 0, 0))],
        out_specs=pl.BlockSpec((None, n_heads, LANES), lambda b, c: (b, 0, 0)),
        out_shape=jax.ShapeDtypeStruct((batch, n_heads, LANES), F32),
        compiler_params=_params("parallel", "arbitrary"),
        name="sb_key_norms",
    )(cache_k)


def _sb_sample_recent_kernel(q_ref, kl_ref, vl_ref, kn_ref, vn_ref, ksq_ref, tri_ref, g_ref, o_ref, bound_ref,
                             *, head_dim):
    p = pl.program_id(1)
    tq, width = q_ref.shape
    q2 = _sb_queries(q_ref[...], head_dim)
    n = q2.shape[0]
    r = lax.broadcasted_iota(jnp.int32, (tq, tq), 0)
    c = lax.broadcasted_iota(jnp.int32, (tq, tq), 1)
    state = _sb_update(q2, kn_ref[...], vn_ref[...], (c >= r).astype(BF16),
                       jnp.zeros((1, n), F32), jnp.zeros((width, n), F32), _sb_before(tq, tq, n))
    carry, acc = _sb_update(q2, kl_ref[...].astype(BF16), vl_ref[...].astype(BF16), tri_ref[...], *state, None)
    hpg = width // head_dim
    lane_head = lax.broadcasted_iota(jnp.int32, (1, n), 1) // tq
    kmax = jnp.zeros((1, n), F32)
    for h in range(hpg):
        ksq = ksq_ref[pl.ds(p * hpg + h, 1), :][:, 0:1]
        kmax = jnp.where(lane_head == h, jnp.sqrt(ksq) * NORM_MARGIN, kmax)
    bound = jnp.max(_sb_query_norms(q2) * kmax - carry, axis=1, keepdims=True)
    bound_ref[...] = jnp.broadcast_to(bound, bound_ref.shape)
    _sb_finish(acc, g_ref, o_ref, head_dim, tq)


def _sb_sample(q_sb, cache_k, cache_v, k_sb, v_sb, g_col, *, tk, head_dim):
    t, sbw = q_sb.shape
    batch, n_past, n_heads, _ = cache_k.shape
    tq = t // batch
    n_grp = sbw // LANES
    assert n_past % tk == 0
    ksq = _key_norms(cache_k, tkn=4 * tk)
    k_last = cache_k[:, n_past - tk:].reshape(batch, tk, sbw)
    v_last = cache_v[:, n_past - tk:].reshape(batch, tk, sbw)
    new = pl.BlockSpec((tq, LANES), lambda b, p: (b, p))
    last = pl.BlockSpec((None, tk, LANES), lambda b, p: (b, 0, p))
    o_fast, bound = pl.pallas_call(
        functools.partial(_sb_sample_recent_kernel, head_dim=head_dim),
        grid=(batch, n_grp),
        in_specs=[new, last, last, new, new,
                  pl.BlockSpec((None, n_heads, LANES), lambda b, p: (b, 0, 0)),
                  pl.BlockSpec((tk, tk), lambda b, p: (0, 0)),
                  pl.BlockSpec((LANES, 1), lambda b, p: (p, 0))],
        out_specs=(new, pl.BlockSpec((None, None, SUBLANES, LANES), lambda b, p: (b, p, 0, 0))),
        out_shape=(jax.ShapeDtypeStruct((t, sbw), BF16),
                   jax.ShapeDtypeStruct((batch, n_grp, SUBLANES, LANES), F32)),
        compiler_params=_params("parallel", "parallel"),
        name="sb_sample",
    )(q_sb, k_last, v_last, k_sb, v_sb, ksq, _triu(tk), g_col)

    def full():
        return _sb_sample_full(q_sb, cache_k.reshape(batch, n_past, sbw), cache_v.reshape(batch, n_past, sbw),
                               k_sb, v_sb, g_col, tk=tk, head_dim=head_dim)

    return lax.cond(jnp.max(bound) > DEAD_EXPONENT, full, lambda: o_fast)


def _out_proj_kernel(oa_ref, ob_ref, h_ref, wa_ref, wb_ref, g_ref, o_ref):
    m = _dot(oa_ref[...], wa_ref[...]) + _dot(ob_ref[...], wb_ref[...])
    o_ref[...] = h_ref[...] + _rms(m, g_ref[...])


def _out_proj(o_a, o_b, h, w_a, w_b, g_post, *, tm):
    t, d = h.shape
    row = lambda i: (i, 0)
    const = lambda i: (0, 0)
    return pl.pallas_call(
        _out_proj_kernel,
        grid=(t // tm,),
        in_specs=[
            pl.BlockSpec((tm, o_a.shape[1]), row),
            pl.BlockSpec((tm, o_b.shape[1]), row),
            pl.BlockSpec((tm, d), row),
            pl.BlockSpec(w_a.shape, const),
            pl.BlockSpec(w_b.shape, const),
            pl.BlockSpec((1, d), const),
        ],
        out_specs=pl.BlockSpec((tm, d), row),
        out_shape=jax.ShapeDtypeStruct((t, d), F32),
        compiler_params=_params("parallel"),
        name="out_proj",
    )(o_a, o_b, h, w_a, w_b, g_post)


def _rot_cols(w):
    half = w.shape[-1] // 2
    return jnp.concatenate([-w[..., half:], w[..., :half]], axis=-1)


def _rope_tables(pos, rope_dim, rows):
    half = rope_dim // 2
    inv_freq = ROPE_THETA ** (-jnp.arange(half, dtype=F32) / half)
    ang = pos.astype(F32)[:, None] * inv_freq[None, :]
    pad = jnp.zeros((pos.shape[0], LANES - rope_dim), F32)
    cosp = jnp.concatenate([jnp.cos(ang), jnp.cos(ang), pad], axis=1)
    sinp = jnp.concatenate([jnp.sin(ang), jnp.sin(ang), pad], axis=1)
    reps = max(1, rows // pos.shape[0])
    return jnp.tile(cosp, (reps, 1)), jnp.tile(sinp, (reps, 1))


def kernel(x_prompt, x_sample, cache_mla_latent, cache_mla_krope, cache_sb_k, cache_sb_v, g_pre_ff1, w_gate1, w_up1, w_down1, g_post_ff1, g_pre_mix, w_in, g_q, w_uq, g_kv, w_uk, w_uv, g_mla_out, g_sb_out, w_out, g_post_mix, g_pre_ff2, w_gate2, w_up2, w_down2, g_post_ff2, g_final):
    depth = w_in.shape[0]
    assert depth == 1
    batch, seq, d = x_prompt.shape
    dec_batch, dec_seq, _ = x_sample.shape
    n_past = cache_mla_latent.shape[2]
    _, q_lora, heads, qk_dim = w_uq.shape
    kv_lora, _, nope = w_uk.shape[1:]
    rope_dim = qk_dim - nope
    v_dim = w_uv.shape[3]
    sb_heads, sb_dim = g_sb_out.shape[1:]
    sbw = sb_heads * sb_dim
    mla_scale = float(qk_dim) ** -0.5
    sb_scale = float(sb_dim) ** -0.5
    assert kv_lora == LANES and 2 * rope_dim <= LANES and LANES % sb_dim == 0
    dims = (heads, q_lora, kv_lora, rope_dim, sbw, sb_heads, mla_scale * LOG2E, sb_scale)

    row = lambda g: g.reshape(1, -1).astype(F32)
    l = 0
    wi = w_in[l]
    i1 = q_lora + kv_lora
    i2 = i1 + rope_dim
    w_kr = wi[:, i1:i2]
    w_in_ext = jnp.concatenate(
        [wi[:, :i1], w_kr, _rot_cols(w_kr), jnp.zeros((d, LANES - 2 * rope_dim), F32), wi[:, i2:]], axis=1).astype(BF16)
    wq = w_uq[l]
    wq_rope = wq[:, :, nope:]
    wq_grp = jnp.concatenate(
        [wq_rope, _rot_cols(wq_rope), jnp.zeros((q_lora, heads, LANES - 2 * rope_dim), F32)], axis=2)
    w_uq_ext = jnp.concatenate(
        [wq[:, :, :nope].reshape(q_lora, heads * nope), wq_grp.reshape(q_lora, heads * LANES)], axis=1).astype(BF16)
    wk = jnp.transpose(w_uk[l], (1, 2, 0))
    eye = jnp.eye(heads, dtype=F32)
    w_uk_bd = (wk[:, :, None, :] * eye[:, None, :, None]).reshape(heads * nope, heads * kv_lora).astype(BF16)
    w_uv_t = jnp.transpose(w_uv[l], (1, 2, 0)).astype(BF16)
    w_out_a = w_out[l][:heads * v_dim].astype(BF16)
    w_out_b = w_out[l][heads * v_dim:].astype(BF16)
    ffn1 = (row(g_pre_ff1[l]), w_gate1[l].astype(BF16), w_up1[l].astype(BF16), w_down1[l].astype(BF16),
            row(g_post_ff1[l]), row(g_final[l]))
    ffn2 = (row(g_pre_ff2[l]), w_gate2[l].astype(BF16), w_up2[l].astype(BF16), w_down2[l].astype(BF16),
            row(g_post_ff2[l]), row(g_final[l]))
    g_mla = g_mla_out[l].reshape(heads, v_dim, 1).astype(F32)
    g_sb = g_sb_out[l].reshape(sbw, 1).astype(F32)

    def layer(x, pos, tm, attend):
        h = _ffn(x, *ffn1, final=False, tm=tm)
        cosp, sinp = _rope_tables(pos, rope_dim, tm)
        lat, kr, kn, vn, qcat, kcat, lat_t, q_sb, k_sb, v_sb, ksq = _mix_in(
            h, row(g_pre_mix[l]), w_in_ext, row(g_q[l]), w_uq_ext, w_uk_bd, row(g_kv[l]), cosp, sinp,
            tm=tm, dims=dims)
        o_mla, o_sb = attend(qcat, kcat, lat_t, q_sb, k_sb, v_sb, ksq)
        h = _out_proj(o_mla, o_sb, h, w_out_a, w_out_b, row(g_post_mix[l]), tm=tm)
        y = _ffn(h, *ffn2, final=True, tm=tm)
        return y, (lat, kr, kn, vn)

    def attend_prompt(qcat, kcat, lat_t, q_sb, k_sb, v_sb, ksq):
        o_mla = _mla_prompt(qcat, kcat, lat_t, w_uv_t, g_mla, batch=batch, seq=seq, tq=512, tk=512)
        o_sb = _sb_prompt(q_sb, k_sb, v_sb, ksq, g_sb, batch=batch, seq=seq, tq=256, head_dim=sb_dim, gw=2 * LANES)
        return o_mla, o_sb

    def attend_sample(qcat, kcat, lat_t, q_sb, k_sb, v_sb, ksq):
        o_mla = _mla_sample(qcat, cache_mla_latent[l], cache_mla_krope[l], kcat, w_uv_t, g_mla, tk=512)
        o_sb = _sb_sample(q_sb, cache_sb_k[l], cache_sb_v[l], k_sb, v_sb, g_sb, tk=256, head_dim=sb_dim)
        return o_mla, o_sb

    pos_p = jnp.arange(seq, dtype=jnp.int32)
    pos_s = n_past + jnp.arange(dec_seq, dtype=jnp.int32)
    yp, rp = layer(x_prompt.reshape(batch * seq, d), pos_p, 512, attend_prompt)
    ys, rs = layer(x_sample.reshape(dec_batch * dec_seq, d), pos_s, 512, attend_sample)

    def rows(r, b, s):
        lat, kr, kn, vn = r
        return (lat.reshape(1, b, s, kv_lora), kr.reshape(1, b, s, rope_dim),
                kn.reshape(1, b, s, sb_heads, sb_dim), vn.reshape(1, b, s, sb_heads, sb_dim))

    return (yp.reshape(batch, seq, d), ys.reshape(dec_batch, dec_seq, d)) + rows(rp, batch, seq) + rows(rs, dec_batch, dec_seq)
```

```python
import functools

import jax
import jax.numpy as jnp
from jax import lax
from jax.experimental import pallas as pl
from jax.experimental.pallas import tpu as pltpu

EPS = 1e-6
CHUNK = 64
ROPE_THETA = 10000.0
LANES = 128
F32 = jnp.float32
BF16 = jnp.bfloat16
VMEM_LIMIT = 56 * 1024 * 1024
SUBLANES = 8
MXU_WIDTH = 256
ONES_ROWS = 2 * SUBLANES
LOG2E = 1.4426950408889634
DEAD_EXPONENT = -120.0
NORM_MARGIN = 1.02
MLA_SKEW = (2, 3)
MLA_GROUP_LANES = 512


def _dot(a, b):
    return jnp.dot(a, b, preferred_element_type=F32)


def _dot_nt(a, b):
    return lax.dot_general(a, b, (((1,), (1,)), ((), ())), preferred_element_type=F32)


def _dot_tn(a, b):
    return lax.dot_general(a, b, (((0,), (0,)), ((), ())), preferred_element_type=F32)


def _rms(x, g):
    return x * lax.rsqrt(jnp.mean(x * x, axis=-1, keepdims=True) + EPS) * g


def _params(*sem):
    return pltpu.CompilerParams(dimension_semantics=sem, vmem_limit_bytes=VMEM_LIMIT)


def _ffn_kernel(x_ref, gpre_ref, wg_ref, wu_ref, wd_ref, gpost_ref, gfin_ref, o_ref, *, final, fc):
    x = x_ref[...]
    xn = _rms(x, gpre_ref[...]).astype(BF16)
    d_ff = wg_ref.shape[1]
    acc = jnp.zeros(x.shape, F32)
    for c in range(0, d_ff, fc):
        g = _dot(xn, wg_ref[:, c:c + fc])
        u = _dot(xn, wu_ref[:, c:c + fc])
        a = (g * jax.nn.sigmoid(g)) * u
        acc = acc + _dot(a.astype(BF16), wd_ref[c:c + fc, :])
    h = x + 0.5 * _rms(acc, gpost_ref[...])
    if final:
        h = _rms(h, gfin_ref[...])
    o_ref[...] = h


def _ffn(x, g_pre, wg, wu, wd, g_post, g_fin, *, final, tm):
    t, d = x.shape
    d_ff = wg.shape[1]
    fc = MXU_WIDTH
    assert d_ff % fc == 0
    const = lambda i: (0, 0)
    return pl.pallas_call(
        functools.partial(_ffn_kernel, final=final, fc=fc),
        grid=(t // tm,),
        in_specs=[
            pl.BlockSpec((tm, d), lambda i: (i, 0)),
            pl.BlockSpec((1, d), const),
            pl.BlockSpec((d, d_ff), const, pipeline_mode=pl.Buffered(1)),
            pl.BlockSpec((d, d_ff), const, pipeline_mode=pl.Buffered(1)),
            pl.BlockSpec((d_ff, d), const, pipeline_mode=pl.Buffered(1)),
            pl.BlockSpec((1, d), const),
            pl.BlockSpec((1, d), const),
        ],
        out_specs=pl.BlockSpec((tm, d), lambda i: (i, 0)),
        out_shape=jax.ShapeDtypeStruct((t, d), F32),
        compiler_params=_params("parallel"),
        name="ffn_final" if final else "ffn",
    )(x, g_pre, wg, wu, wd, g_post, g_fin)


def _mix_in_kernel(h_ref, gpre_ref, win_ref, gq_ref, wuq_ref, wuk_ref, gkv_ref, cos_ref, sin_ref, sel_ref,
                   lat_ref, kr_ref, kn_ref, vn_ref, qcat_ref, kcat_ref, latt_ref, qsb_ref, ksb_ref, vsb_ref, ksq_ref,
                   *, heads, q_lora, kv_lora, rope_dim, sbw, mla_scale, sb_scale):
    u = _rms(h_ref[...], gpre_ref[...]).astype(BF16)
    proj = _dot(u, win_ref[...])
    cosp = cos_ref[...]
    sinp = sin_ref[...]
    o1 = q_lora
    o2 = o1 + kv_lora
    o3 = o2 + LANES

    def rope(grp):
        return grp * cosp + pltpu.roll(grp, LANES - rope_dim, 1) * sinp

    latent = _rms(proj[:, o1:o2], gkv_ref[...])
    krope = rope(proj[:, o2:o3])
    lat_ref[...] = latent
    kr_ref[...] = krope[:, :rope_dim]
    k_sb = proj[:, o3 + sbw:o3 + 2 * sbw]
    v_sb = proj[:, o3 + 2 * sbw:o3 + 3 * sbw]
    kn_ref[...] = k_sb
    vn_ref[...] = v_sb
    k_bf = k_sb.astype(BF16)
    ksb_ref[...] = k_bf
    vsb_ref[...] = v_sb.astype(BF16)
    qsb_ref[...] = (proj[:, o3:o3 + sbw] * sb_scale).astype(BF16)
    kcat_ref[...] = jnp.concatenate([latent, krope], axis=1).astype(BF16)
    latt_ref[...] = jnp.concatenate([latent.T, jnp.ones((ONES_ROWS, latent.shape[0]), F32)], axis=0).astype(BF16)
    k_f = k_bf.astype(F32)
    ksq_ref[...] = _dot((k_f * k_f).astype(BF16), sel_ref[...])

    c_q = _rms(proj[:, :o1], gq_ref[...]).astype(BF16)
    q = _dot(c_q, wuq_ref[...])
    n_nope = wuk_ref.shape[0]
    q_lat = _dot(q[:, :n_nope].astype(BF16), wuk_ref[...])
    for hd in range(heads):
        ql = q_lat[:, hd * kv_lora:(hd + 1) * kv_lora] * mla_scale
        qr = rope(q[:, n_nope + hd * LANES:n_nope + (hd + 1) * LANES]) * mla_scale
        qcat_ref[hd] = jnp.concatenate([ql, qr], axis=1).astype(BF16)


def _mix_in(h, g_pre, w_in_ext, g_q, w_uq_ext, w_uk_bd, g_kv, cosp, sinp, *, tm, dims):
    t, d = h.shape
    heads, q_lora, kv_lora, rope_dim, sbw, sb_heads, mla_scale, sb_scale = dims
    n_tab = cosp.shape[0] // tm
    const = lambda i: (0, 0)
    row = lambda i: (i, 0)
    tab = lambda i: (i % n_tab, 0)
    out_shape = (
        jax.ShapeDtypeStruct((t, kv_lora), F32),
        jax.ShapeDtypeStruct((t, rope_dim), F32),
        jax.ShapeDtypeStruct((t, sbw), F32),
        jax.ShapeDtypeStruct((t, sbw), F32),
        jax.ShapeDtypeStruct((heads, t, 2 * LANES), BF16),
        jax.ShapeDtypeStruct((t, 2 * LANES), BF16),
        jax.ShapeDtypeStruct((kv_lora + ONES_ROWS, t), BF16),
        jax.ShapeDtypeStruct((t, sbw), BF16),
        jax.ShapeDtypeStruct((t, sbw), BF16),
        jax.ShapeDtypeStruct((t, sbw), BF16),
        jax.ShapeDtypeStruct((t, LANES), F32),
    )
    out_specs = (
        pl.BlockSpec((tm, kv_lora), row),
        pl.BlockSpec((tm, rope_dim), row),
        pl.BlockSpec((tm, sbw), row),
        pl.BlockSpec((tm, sbw), row),
        pl.BlockSpec((heads, tm, 2 * LANES), lambda i: (0, i, 0)),
        pl.BlockSpec((tm, 2 * LANES), row),
        pl.BlockSpec((kv_lora + ONES_ROWS, tm), lambda i: (0, i)),
        pl.BlockSpec((tm, sbw), row),
        pl.BlockSpec((tm, sbw), row),
        pl.BlockSpec((tm, sbw), row),
        pl.BlockSpec((tm, LANES), row),
    )
    sel = (lax.broadcasted_iota(jnp.int32, (sbw, LANES), 0) // (sbw // sb_heads)
           == lax.broadcasted_iota(jnp.int32, (sbw, LANES), 1)).astype(BF16)
    return pl.pallas_call(
        functools.partial(_mix_in_kernel, heads=heads, q_lora=q_lora, kv_lora=kv_lora,
                          rope_dim=rope_dim, sbw=sbw, mla_scale=mla_scale, sb_scale=sb_scale),
        grid=(t // tm,),
        in_specs=[
            pl.BlockSpec((tm, d), row),
            pl.BlockSpec((1, d), const),
            pl.BlockSpec(w_in_ext.shape, const),
            pl.BlockSpec((1, q_lora), const),
            pl.BlockSpec(w_uq_ext.shape, const),
            pl.BlockSpec(w_uk_bd.shape, const),
            pl.BlockSpec((1, kv_lora), const),
            pl.BlockSpec((tm, LANES), tab),
            pl.BlockSpec((tm, LANES), tab),
            pl.BlockSpec((sbw, LANES), const),
        ],
        out_specs=out_specs,
        out_shape=out_shape,
        compiler_params=_params("parallel"),
        name="mix_in",
    )(h, g_pre, w_in_ext, g_q, w_uq_ext, w_uk_bd, g_kv, cosp, sinp, sel)


def _mla_scores(qs, ks):
    return sum(_dot_nt(kk, qq) for qq, kk in zip(qs, ks))


def _mla_probs(s, m_ref, visible):
    if visible is not None:
        s = jnp.where(visible, s, -jnp.inf)
    m_prev = m_ref[...]
    m_new = jnp.maximum(m_prev, jnp.max(s, axis=0, keepdims=True))
    m_ref[...] = m_new
    return jnp.exp2(m_prev - m_new), jnp.exp2(s - m_new)


def _mla_accumulate(alpha, p, v, acc_ref, v_transposed):
    pb = p.astype(BF16)
    if v_transposed:
        pv = _dot(v, pb)
    else:
        l = jnp.sum(p, axis=0, keepdims=True)
        pv = jnp.concatenate([_dot_tn(v, pb), jnp.broadcast_to(l, (ONES_ROWS, l.shape[1]))], axis=0)
    acc_ref[...] = alpha * acc_ref[...] + pv


def _mla_update(s, v, m_ref, acc_ref, visible, v_transposed):
    alpha, p = _mla_probs(s, m_ref, visible)
    _mla_accumulate(alpha, p, v, acc_ref, v_transposed)


def _own_block_rms_t(x, g_col, head_rows, tq):
    r = lax.broadcasted_iota(jnp.int32, x.shape, 0) // head_rows
    c = lax.broadcasted_iota(jnp.int32, x.shape, 1) // tq
    o = jnp.where(r == c, x, 0.0)
    inv = lax.rsqrt(jnp.sum(o * o, axis=0, keepdims=True) / head_rows + EPS)
    o_t = (o * inv * g_col).T
    out = o_t[:tq]
    for h in range(1, x.shape[1] // tq):
        out = out + o_t[h * tq:(h + 1) * tq]
    return out


def _mla_finish(acc_ref, wuvt_ref, g_ref, o_ref, heads, tq):
    c_dim = acc_ref.shape[0] - ONES_ROWS
    v_dim = wuvt_ref.shape[1]
    o_lat = (acc_ref[:c_dim, :] / acc_ref[c_dim:c_dim + 1, :]).astype(BF16)
    if tq % LANES == 0:
        outs = []
        for hd in range(heads):
            o = _dot(wuvt_ref[hd], o_lat[:, hd * tq:(hd + 1) * tq])
            outs.append(o * lax.rsqrt(jnp.mean(o * o, axis=0, keepdims=True) + EPS) * g_ref[hd])
        o_t = jnp.concatenate(outs, axis=0).T
    else:
        full = _dot(wuvt_ref[...].reshape(heads * v_dim, c_dim), o_lat)
        o_t = _own_block_rms_t(full, g_ref[...].reshape(heads * v_dim, 1), v_dim, tq)
    o_ref[...] = o_t.astype(BF16)


def _mla_init(m_ref, acc_ref):
    m_ref[...] = jnp.full(m_ref.shape, -jnp.inf, F32)
    acc_ref[...] = jnp.zeros(acc_ref.shape, F32)


def _mla_prompt_kernel(q_ref, k_ref, vt_ref, wuvt_ref, g_ref, o_ref, m_ref, acc_ref, *, heads, tq, tk):
    j = pl.program_id(1)
    q0 = j * tq
    q = q_ref[...].reshape(heads * tq, q_ref.shape[2])
    _mla_init(m_ref, acc_ref)
    n_full = q0 // tk

    def blocks(todo):
        lanes = [slice(r0, r0 + MLA_GROUP_LANES) for r0 in range(0, heads * tq, MLA_GROUP_LANES)]
        units = [(kb, vis, ln) for kb, vis in todo for ln in lanes]

        def rows(kb):
            return pl.ds(pl.multiple_of(kb * tk, tk), tk)

        s, ap = {}, {}
        lag_p, lag_a = MLA_SKEW
        for step in range(len(units) + lag_a):
            if step < len(units):
                kb, _, ln = units[step]
                s[step] = _dot_nt(k_ref[rows(kb), :], q[ln])
            u = step - lag_p
            if 0 <= u < len(units):
                kb, vis, ln = units[u]
                ap[u] = _mla_probs(s.pop(u), m_ref.at[:, ln], None if vis is None else vis[:, ln])
            u = step - lag_a
            if 0 <= u < len(units):
                kb, _, ln = units[u]
                _mla_accumulate(*ap.pop(u), vt_ref[:, rows(kb)], acc_ref.at[:, ln], True)

    def full_block(kb, carry):
        blocks([(kb, None)])
        return carry

    lax.fori_loop(0, n_full, full_block, 0)
    kpos = n_full * tk + lax.broadcasted_iota(jnp.int32, (tk, heads * tq), 0)
    qpos = q0 + lax.broadcasted_iota(jnp.int32, (tk, heads * tq), 1) % tq
    blocks([(n_full, (kpos // CHUNK) <= (qpos // CHUNK))])
    _mla_finish(acc_ref, wuvt_ref, g_ref, o_ref, heads, tq)


def _mla_scratch(rows, c_dim):
    return [pltpu.VMEM((1, rows), F32), pltpu.VMEM((c_dim + ONES_ROWS, rows), F32)]


def _mla_prompt(qcat, kcat, lat_t, w_uv_t, g_out, *, batch, seq, tq, tk):
    heads, t, dk = qcat.shape
    v_dim, c_dim = w_uv_t.shape[1], w_uv_t.shape[2]
    assert tk % tq == 0 and tq % CHUNK == 0 and seq % tk == 0 and tq % LANES == 0
    nq = seq // tq
    return pl.pallas_call(
        functools.partial(_mla_prompt_kernel, heads=heads, tq=tq, tk=tk),
        grid=(batch, nq),
        in_specs=[
            pl.BlockSpec((heads, tq, dk), lambda b, j: (0, b * nq + j, 0)),
            pl.BlockSpec((seq, dk), lambda b, j: (b, 0)),
            pl.BlockSpec((lat_t.shape[0], seq), lambda b, j: (0, b)),
            pl.BlockSpec(w_uv_t.shape, lambda b, j: (0, 0, 0)),
            pl.BlockSpec(g_out.shape, lambda b, j: (0, 0, 0)),
        ],
        out_specs=pl.BlockSpec((tq, heads * v_dim), lambda b, j: (b * nq + j, 0)),
        out_shape=jax.ShapeDtypeStruct((t, heads * v_dim), BF16),
        scratch_shapes=_mla_scratch(heads * tq, c_dim),
        compiler_params=_params("parallel", "arbitrary"),
        name="mla_prompt",
    )(qcat, kcat, lat_t, w_uv_t, g_out)


def _mla_sample_kernel(q_ref, lat_ref, kr_ref, knew_ref, wuvt_ref, g_ref, o_ref, m_ref, acc_ref,
                       *, heads, tq, tk, c_dim, rope_dim):
    q = q_ref[...].reshape(heads * tq, q_ref.shape[2])
    q_lat = q[:, :c_dim]
    q_rope = q[:, c_dim:c_dim + rope_dim]
    _mla_init(m_ref, acc_ref)
    n_blk = lat_ref.shape[0] // tk

    def latent(kb):
        return lat_ref[pl.ds(pl.multiple_of(kb * tk, tk), tk), :].astype(BF16)

    def scores(kb):
        kr = kr_ref[pl.ds(pl.multiple_of(kb * tk, tk), tk), :].astype(BF16)
        return _mla_scores([q_lat, q_rope], [latent(kb), kr])

    def past_block(kb, s):
        s_next = scores(kb + 1)
        _mla_update(s, latent(kb), m_ref, acc_ref, None, False)
        return s_next

    s = lax.fori_loop(0, n_blk - 1, past_block, scores(0))
    knew = knew_ref[...]
    s_new = _mla_scores([q], [knew])
    _mla_update(s, latent(n_blk - 1), m_ref, acc_ref, None, False)
    _mla_update(s_new, knew[:, :c_dim], m_ref, acc_ref, None, False)
    _mla_finish(acc_ref, wuvt_ref, g_ref, o_ref, heads, tq)


def _mla_sample(qcat, cache_lat, cache_kr, kcat, w_uv_t, g_out, *, tk):
    heads, t, dk = qcat.shape
    batch, n_past, c_dim = cache_lat.shape
    rope_dim = cache_kr.shape[2]
    tq = qh = t // batch
    v_dim = w_uv_t.shape[1]
    assert n_past % CHUNK == 0 and qh <= CHUNK and n_past % tk == 0 and (heads * tq) % LANES == 0
    return pl.pallas_call(
        functools.partial(_mla_sample_kernel, heads=heads, tq=tq, tk=tk, c_dim=c_dim, rope_dim=rope_dim),
        grid=(batch,),
        in_specs=[
            pl.BlockSpec((heads, qh, dk), lambda b: (0, b, 0)),
            pl.BlockSpec((None, n_past, c_dim), lambda b: (b, 0, 0)),
            pl.BlockSpec((None, n_past, rope_dim), lambda b: (b, 0, 0)),
            pl.BlockSpec((qh, dk), lambda b: (b, 0)),
            pl.BlockSpec(w_uv_t.shape, lambda b: (0, 0, 0)),
            pl.BlockSpec(g_out.shape, lambda b: (0, 0, 0)),
        ],
        out_specs=pl.BlockSpec((qh, heads * v_dim), lambda b: (b, 0)),
        out_shape=jax.ShapeDtypeStruct((t, heads * v_dim), BF16),
        scratch_shapes=_mla_scratch(heads * tq, c_dim),
        compiler_params=_params("parallel"),
        name="mla_sample",
    )(qcat, cache_lat, cache_kr, kcat, w_uv_t, g_out)


def _sb_update(q2, kblk, vblk, triu, carry, acc, before, skip_dead=False):
    z = _dot_nt(kblk, q2)
    sub = triu.shape[0]

    def live():
        sp = jnp.maximum(z, 0.0) + jnp.log(1.0 + jnp.exp(-jnp.abs(z)))
        if before is not None:
            sp = jnp.where(before, sp, 0.0)
        hi = sp.astype(BF16)
        lo = (sp - hi.astype(F32)).astype(BF16)
        cs, run = [], carry
        for r0 in reversed(range(0, z.shape[0], sub)):
            cs.insert(0, _dot(triu, hi[r0:r0 + sub]) + _dot(triu, lo[r0:r0 + sub]) + run)
            run = cs[0][0:1, :]
        c = cs[0] if len(cs) == 1 else jnp.concatenate(cs, axis=0)
        a = jnp.exp(z - c)
        if before is not None:
            a = jnp.where(before, a, 0.0)
        return run, acc + _dot_tn(vblk() if callable(vblk) else vblk, a.astype(BF16))

    if not skip_dead:
        return live()
    return lax.cond(jnp.max(z - carry) > DEAD_EXPONENT, live, lambda: (carry, acc))


def _sb_queries(q, head_dim):
    lane = lax.broadcasted_iota(jnp.int32, (1, q.shape[1]), 1) // head_dim
    return jnp.concatenate([jnp.where(lane == h, q, jnp.zeros_like(q)) for h in range(q.shape[1] // head_dim)], axis=0)


def _sb_query_norms(q2):
    q2f = q2.astype(F32)
    return jnp.sqrt(_dot_nt(jnp.ones((SUBLANES, q2.shape[1]), BF16), (q2f * q2f).astype(BF16))[0:1, :]) * NORM_MARGIN


def _sb_before(tk, tq, n, first_query_row=0):
    key = lax.broadcasted_iota(jnp.int32, (tk, n), 0) - first_query_row
    qry = lax.broadcasted_iota(jnp.int32, (tk, n), 1) % tq
    return key < qry


def _sb_finish(acc, g_ref, o_ref, head_dim, tq):
    o_ref[...] = _own_block_rms_t(acc, g_ref[...], head_dim, tq).astype(BF16)


def _sb_prompt_kernel(kmax_ref, q_ref, k_ref, v_ref, tri_ref, g_ref, o_ref, *, tq, head_dim):
    b, p, j = pl.program_id(0), pl.program_id(1), pl.program_id(2)
    n_blk = pl.num_programs(2)
    q2 = _sb_queries(q_ref[...], head_dim)
    width = q_ref.shape[1]
    n = q2.shape[0]
    hpg = width // head_dim
    triu = tri_ref[...]
    zero = (jnp.zeros((1, n), F32), jnp.zeros((width, n), F32))

    def own_block():
        return _sb_update(q2, k_ref[pl.ds(0, tq), :], v_ref[pl.ds(0, tq), :], triu, *zero, _sb_before(tq, tq, n))

    def own_and_previous_block():
        k0 = pl.multiple_of((j - 1) * tq, tq)
        return _sb_update(q2, k_ref[pl.ds(k0, 2 * tq), :], v_ref[pl.ds(k0, 2 * tq), :], triu, *zero,
                          _sb_before(2 * tq, tq, n, first_query_row=tq))

    carry, acc = lax.cond(j > 0, own_and_previous_block, own_block)
    qn = _sb_query_norms(q2)
    lane_head = lax.broadcasted_iota(jnp.int32, (1, n), 1) // tq

    def alive(kb, carry):
        base = (b * n_blk + jnp.maximum(kb, 0)) * (hpg * pl.num_programs(1)) + p * hpg
        kmax = jnp.zeros((1, n), F32)
        for h in range(hpg):
            kmax = jnp.where(lane_head == h, kmax_ref[base + h], kmax)
        return jnp.logical_and(kb >= 0, jnp.max(qn * kmax - carry) > DEAD_EXPONENT)

    def cond(state):
        return state[1]

    def earlier_block(state):
        kb, _, carry, acc = state
        k0 = pl.multiple_of(kb * tq, tq)
        carry, acc = _sb_update(q2, k_ref[pl.ds(k0, tq), :], v_ref[pl.ds(k0, tq), :], triu, carry, acc, None)
        return kb - 1, alive(kb - 1, carry), carry, acc

    state = lax.while_loop(cond, earlier_block, (j - 2, alive(j - 2, carry), carry, acc))
    _sb_finish(state[3], g_ref, o_ref, head_dim, tq)


def _triu(n):
    r = lax.broadcasted_iota(jnp.int32, (n, n), 0)
    c = lax.broadcasted_iota(jnp.int32, (n, n), 1)
    return (c >= r).astype(BF16)


def _sb_prompt(q_sb, k_sb, v_sb, ksq, g_col, *, batch, seq, tq, head_dim, gw):
    t, sbw = q_sb.shape
    nq = seq // tq
    n_grp = sbw // gw
    n_heads = sbw // head_dim
    blk_max = jnp.max(ksq[:, :n_heads].reshape(batch, nq, tq, n_heads), axis=2)
    kmax = (jnp.sqrt(lax.cummax(blk_max, axis=1)) * NORM_MARGIN).reshape(-1)
    return pl.pallas_call(
        functools.partial(_sb_prompt_kernel, tq=tq, head_dim=head_dim),
        grid_spec=pltpu.PrefetchScalarGridSpec(
            num_scalar_prefetch=1,
            grid=(batch, n_grp, nq),
            in_specs=[
                pl.BlockSpec((tq, gw), lambda b, p, j, kmax: (b * nq + j, p)),
                pl.BlockSpec((seq, gw), lambda b, p, j, kmax: (b, p)),
                pl.BlockSpec((seq, gw), lambda b, p, j, kmax: (b, p)),
                pl.BlockSpec((tq, tq), lambda b, p, j, kmax: (0, 0)),
                pl.BlockSpec((gw, 1), lambda b, p, j, kmax: (p, 0)),
            ],
            out_specs=pl.BlockSpec((tq, gw), lambda b, p, j, kmax: (b * nq + j, p)),
        ),
        out_shape=jax.ShapeDtypeStruct((t, sbw), BF16),
        compiler_params=_params("parallel", "parallel", "arbitrary"),
        name="sb_prompt",
    )(kmax, q_sb, k_sb, v_sb, _triu(tq), g_col)


def _sb_sample_kernel(q_ref, kc_ref, vc_ref, kn_ref, vn_ref, tri_ref, g_ref, o_ref, *, tk, head_dim):
    tq, width = q_ref.shape
    q2 = _sb_queries(q_ref[...], head_dim)
    n = q2.shape[0]
    r = lax.broadcasted_iota(jnp.int32, (tq, tq), 0)
    c = lax.broadcasted_iota(jnp.int32, (tq, tq), 1)
    state = _sb_update(q2, kn_ref[...], vn_ref[...], (c >= r).astype(BF16),
                       jnp.zeros((1, n), F32), jnp.zeros((width, n), F32), _sb_before(tq, tq, n))
    triu = tri_ref[...]
    n_blk = kc_ref.shape[0] // tk

    def cached_block(kb, state, skip_dead):
        k0 = pl.multiple_of(kb * tk, tk)
        return _sb_update(q2, kc_ref[pl.ds(k0, tk), :].astype(BF16), lambda: vc_ref[pl.ds(k0, tk), :].astype(BF16),
                          triu, state[0], state[1], None, skip_dead=skip_dead)

    state = cached_block(n_blk - 1, state, False)
    rest = (n_blk - 1) * tk
    z_rest = _dot_nt(kc_ref[:rest, :].astype(BF16), q2)

    def walk():
        return lax.fori_loop(0, n_blk - 1, lambda i, st: cached_block(n_blk - 2 - i, st, True), state)

    state = lax.cond(jnp.max(z_rest - state[0]) > DEAD_EXPONENT, walk, lambda: state)
    _sb_finish(state[1], g_ref, o_ref, head_dim, tq)


def _sb_sample_full(q_sb, cache_k, cache_v, k_sb, v_sb, g_col, *, tk, head_dim):
    t, sbw = q_sb.shape
    batch, n_past, _ = cache_k.shape
    tq = t // batch
    n_grp = sbw // LANES
    assert n_past % tk == 0
    new = pl.BlockSpec((tq, LANES), lambda b, p: (b, p))
    past = pl.BlockSpec((None, n_past, LANES), lambda b, p: (b, 0, p))
    return pl.pallas_call(
        functools.partial(_sb_sample_kernel, tk=tk, head_dim=head_dim),
        grid=(batch, n_grp),
        in_specs=[new, past, past, new, new,
                  pl.BlockSpec((tk, tk), lambda b, p: (0, 0)),
                  pl.BlockSpec((LANES, 1), lambda b, p: (p, 0))],
        out_specs=new,
        out_shape=jax.ShapeDtypeStruct((t, sbw), BF16),
        compiler_params=_params("parallel", "parallel"),
        name="sb_sample_full",
    )(q_sb, cache_k, cache_v, k_sb, v_sb, _triu(tk), g_col)


def _sb_sample_recent_kernel(q_ref, kc_ref, vl_ref, kn_ref, vn_ref, tri_ref, g_ref, o_ref, bound_ref, *, head_dim):
    tq, width = q_ref.shape
    tk = vl_ref.shape[0]
    rest = kc_ref.shape[0] - tk
    q2 = _sb_queries(q_ref[...], head_dim)
    n = q2.shape[0]
    r = lax.broadcasted_iota(jnp.int32, (tq, tq), 0)
    c = lax.broadcasted_iota(jnp.int32, (tq, tq), 1)
    state = _sb_update(q2, kn_ref[...], vn_ref[...], (c >= r).astype(BF16),
                       jnp.zeros((1, n), F32), jnp.zeros((width, n), F32), _sb_before(tq, tq, n))
    carry, acc = _sb_update(q2, kc_ref[rest:, :].astype(BF16), vl_ref[...].astype(BF16), tri_ref[...], *state, None)
    z_rest = _dot_nt(kc_ref[:rest, :].astype(BF16), q2)
    bound = jnp.max(jnp.max(z_rest - carry, axis=0, keepdims=True), axis=1, keepdims=True)
    bound_ref[...] = jnp.broadcast_to(bound, bound_ref.shape)
    _sb_finish(acc, g_ref, o_ref, head_dim, tq)


def _sb_sample(q_sb, cache_k, cache_v, k_sb, v_sb, g_col, *, tk, head_dim):
    t, sbw = q_sb.shape
    batch, n_past = cache_k.shape[:2]
    tq = t // batch
    n_grp = sbw // LANES
    assert n_past % tk == 0
    keys = cache_k.reshape(batch, n_past, sbw)
    v_last = cache_v[:, n_past - tk:].reshape(batch, tk, sbw)
    new = pl.BlockSpec((tq, LANES), lambda b, p: (b, p))
    o_fast, bound = pl.pallas_call(
        functools.partial(_sb_sample_recent_kernel, head_dim=head_dim),
        grid=(batch, n_grp),
        in_specs=[new,
                  pl.BlockSpec((None, n_past, LANES), lambda b, p: (b, 0, p)),
                  pl.BlockSpec((None, tk, LANES), lambda b, p: (b, 0, p)),
                  new, new,
                  pl.BlockSpec((tk, tk), lambda b, p: (0, 0)),
                  pl.BlockSpec((LANES, 1), lambda b, p: (p, 0))],
        out_specs=(new, pl.BlockSpec((None, None, SUBLANES, LANES), lambda b, p: (b, p, 0, 0))),
        out_shape=(jax.ShapeDtypeStruct((t, sbw), BF16),
                   jax.ShapeDtypeStruct((batch, n_grp, SUBLANES, LANES), F32)),
        compiler_params=_params("parallel", "parallel"),
        name="sb_sample",
    )(q_sb, keys, v_last, k_sb, v_sb, _triu(tk), g_col)

    def full():
        return _sb_sample_full(q_sb, keys, cache_v.reshape(batch, n_past, sbw), k_sb, v_sb, g_col,
                               tk=tk, head_dim=head_dim)

    return lax.cond(jnp.max(bound) > DEAD_EXPONENT, full, lambda: o_fast)


def _out_proj_kernel(oa_ref, ob_ref, h_ref, wa_ref, wb_ref, g_ref, o_ref):
    m = _dot(oa_ref[...], wa_ref[...]) + _dot(ob_ref[...], wb_ref[...])
    o_ref[...] = h_ref[...] + _rms(m, g_ref[...])


def _out_proj(o_a, o_b, h, w_a, w_b, g_post, *, tm):
    t, d = h.shape
    row = lambda i: (i, 0)
    const = lambda i: (0, 0)
    return pl.pallas_call(
        _out_proj_kernel,
        grid=(t // tm,),
        in_specs=[
            pl.BlockSpec((tm, o_a.shape[1]), row),
            pl.BlockSpec((tm, o_b.shape[1]), row),
            pl.BlockSpec((tm, d), row),
            pl.BlockSpec(w_a.shape, const),
            pl.BlockSpec(w_b.shape, const),
            pl.BlockSpec((1, d), const),
        ],
        out_specs=pl.BlockSpec((tm, d), row),
        out_shape=jax.ShapeDtypeStruct((t, d), F32),
        compiler_params=_params("parallel"),
        name="out_proj",
    )(o_a, o_b, h, w_a, w_b, g_post)


def _rot_cols(w):
    half = w.shape[-1] // 2
    return jnp.concatenate([-w[..., half:], w[..., :half]], axis=-1)


def _rope_tables(pos, rope_dim, rows):
    half = rope_dim // 2
    inv_freq = ROPE_THETA ** (-jnp.arange(half, dtype=F32) / half)
    ang = pos.astype(F32)[:, None] * inv_freq[None, :]
    pad = jnp.zeros((pos.shape[0], LANES - rope_dim), F32)
    cosp = jnp.concatenate([jnp.cos(ang), jnp.cos(ang), pad], axis=1)
    sinp = jnp.concatenate([jnp.sin(ang), jnp.sin(ang), pad], axis=1)
    reps = max(1, rows // pos.shape[0])
    return jnp.tile(cosp, (reps, 1)), jnp.tile(sinp, (reps, 1))


def kernel(x_prompt, x_sample, cache_mla_latent, cache_mla_krope, cache_sb_k, cache_sb_v, g_pre_ff1, w_gate1, w_up1, w_down1, g_post_ff1, g_pre_mix, w_in, g_q, w_uq, g_kv, w_uk, w_uv, g_mla_out, g_sb_out, w_out, g_post_mix, g_pre_ff2, w_gate2, w_up2, w_down2, g_post_ff2, g_final):
    depth = w_in.shape[0]
    assert depth == 1
    batch, seq, d = x_prompt.shape
    dec_batch, dec_seq, _ = x_sample.shape
    n_past = cache_mla_latent.shape[2]
    _, q_lora, heads, qk_dim = w_uq.shape
    kv_lora, _, nope = w_uk.shape[1:]
    rope_dim = qk_dim - nope
    v_dim = w_uv.shape[3]
    sb_heads, sb_dim = g_sb_out.shape[1:]
    sbw = sb_heads * sb_dim
    mla_scale = float(qk_dim) ** -0.5
    sb_scale = float(sb_dim) ** -0.5
    assert kv_lora == LANES and 2 * rope_dim <= LANES and LANES % sb_dim == 0
    dims = (heads, q_lora, kv_lora, rope_dim, sbw, sb_heads, mla_scale * LOG2E, sb_scale)

    row = lambda g: g.reshape(1, -1).astype(F32)
    l = 0
    wi = w_in[l]
    i1 = q_lora + kv_lora
    i2 = i1 + rope_dim
    w_kr = wi[:, i1:i2]
    w_in_ext = jnp.concatenate(
        [wi[:, :i1], w_kr, _rot_cols(w_kr), jnp.zeros((d, LANES - 2 * rope_dim), F32), wi[:, i2:]], axis=1).astype(BF16)
    wq = w_uq[l]
    wq_rope = wq[:, :, nope:]
    wq_grp = jnp.concatenate(
        [wq_rope, _rot_cols(wq_rope), jnp.zeros((q_lora, heads, LANES - 2 * rope_dim), F32)], axis=2)
    w_uq_ext = jnp.concatenate(
        [wq[:, :, :nope].reshape(q_lora, heads * nope), wq_grp.reshape(q_lora, heads * LANES)], axis=1).astype(BF16)
    wk = jnp.transpose(w_uk[l], (1, 2, 0))
    eye = jnp.eye(heads, dtype=F32)
    w_uk_bd = (wk[:, :, None, :] * eye[:, None, :, None]).reshape(heads * nope, heads * kv_lora).astype(BF16)
    w_uv_t = jnp.transpose(w_uv[l], (1, 2, 0)).astype(BF16)
    w_out_a = w_out[l][:heads * v_dim].astype(BF16)
    w_out_b = w_out[l][heads * v_dim:].astype(BF16)
    ffn1 = (row(g_pre_ff1[l]), w_gate1[l].astype(BF16), w_up1[l].astype(BF16), w_down1[l].astype(BF16),
            row(g_post_ff1[l]), row(g_final[l]))
    ffn2 = (row(g_pre_ff2[l]), w_gate2[l].astype(BF16), w_up2[l].astype(BF16), w_down2[l].astype(BF16),
            row(g_post_ff2[l]), row(g_final[l]))
    g_mla = g_mla_out[l].reshape(heads, v_dim, 1).astype(F32)
    g_sb = g_sb_out[l].reshape(sbw, 1).astype(F32)

    def layer(x, pos, tm, attend):
        h = _ffn(x, *ffn1, final=False, tm=tm)
        cosp, sinp = _rope_tables(pos, rope_dim, tm)
        lat, kr, kn, vn, qcat, kcat, lat_t, q_sb, k_sb, v_sb, ksq = _mix_in(
            h, row(g_pre_mix[l]), w_in_ext, row(g_q[l]), w_uq_ext, w_uk_bd, row(g_kv[l]), cosp, sinp,
            tm=tm, dims=dims)
        o_mla, o_sb = attend(qcat, kcat, lat_t, q_sb, k_sb, v_sb, ksq)
        h = _out_proj(o_mla, o_sb, h, w_out_a, w_out_b, row(g_post_mix[l]), tm=tm)
        y = _ffn(h, *ffn2, final=True, tm=tm)
        return y, (lat, kr, kn, vn)

    def attend_prompt(qcat, kcat, lat_t, q_sb, k_sb, v_sb, ksq):
        o_mla = _mla_prompt(qcat, kcat, lat_t, w_uv_t, g_mla, batch=batch, seq=seq, tq=512, tk=512)
        o_sb = _sb_prompt(q_sb, k_sb, v_sb, ksq, g_sb, batch=batch, seq=seq, tq=256, head_dim=sb_dim, gw=2 * LANES)
        return o_mla, o_sb

    def attend_sample(qcat, kcat, lat_t, q_sb, k_sb, v_sb, ksq):
        o_mla = _mla_sample(qcat, cache_mla_latent[l], cache_mla_krope[l], kcat, w_uv_t, g_mla, tk=512)
        o_sb = _sb_sample(q_sb, cache_sb_k[l], cache_sb_v[l], k_sb, v_sb, g_sb, tk=256, head_dim=sb_dim)
        return o_mla, o_sb

    pos_p = jnp.arange(seq, dtype=jnp.int32)
    pos_s = n_past + jnp.arange(dec_seq, dtype=jnp.int32)
    yp, rp = layer(x_prompt.reshape(batch * seq, d), pos_p, 512, attend_prompt)
    ys, rs = layer(x_sample.reshape(dec_batch * dec_seq, d), pos_s, 512, attend_sample)

    def rows(r, b, s):
        lat, kr, kn, vn = r
        return (lat.reshape(1, b, s, kv_lora), kr.reshape(1, b, s, rope_dim),
                kn.reshape(1, b, s, sb_heads, sb_dim), vn.reshape(1, b, s, sb_heads, sb_dim))

    return (yp.reshape(batch, seq, d), ys.reshape(dec_batch, dec_seq, d)) + rows(rp, batch, seq) + rows(rs, dec_batch, dec_seq)
```

```python
import functools

import jax
import jax.numpy as jnp
from jax import lax
from jax.experimental import pallas as pl
from jax.experimental.pallas import tpu as pltpu

EPS = 1e-6
CHUNK = 64
ROPE_THETA = 10000.0
LANES = 128
F32 = jnp.float32
BF16 = jnp.bfloat16
VMEM_LIMIT = 56 * 1024 * 1024
SUBLANES = 8
MXU_WIDTH = 256
ONES_ROWS = 2 * SUBLANES
LOG2E = 1.4426950408889634
DEAD_EXPONENT = -120.0
NORM_MARGIN = 1.02
MLA_SKEW = (2, 3)
MLA_GROUP_LANES = 512


def _dot(a, b):
    return jnp.dot(a, b, preferred_element_type=F32)


def _dot_nt(a, b):
    return lax.dot_general(a, b, (((1,), (1,)), ((), ())), preferred_element_type=F32)


def _dot_tn(a, b):
    return lax.dot_general(a, b, (((0,), (0,)), ((), ())), preferred_element_type=F32)


def _rms(x, g):
    return x * lax.rsqrt(jnp.mean(x * x, axis=-1, keepdims=True) + EPS) * g


def _params(*sem):
    return pltpu.CompilerParams(dimension_semantics=sem, vmem_limit_bytes=VMEM_LIMIT)


def _ffn_kernel(x_ref, *rest, final, fc):
    _ffn_body(x_ref[...], *rest, final=final, fc=fc)


def _ffn_after_mix_kernel(oa_ref, ob_ref, h_ref, wa_ref, wb_ref, gmix_ref, *rest, final, fc):
    m = _dot(oa_ref[...], wa_ref[...]) + _dot(ob_ref[...], wb_ref[...])
    _ffn_body(h_ref[...] + _rms(m, gmix_ref[...]), *rest, final=final, fc=fc)


def _ffn_body(x, gpre_ref, wg_ref, wu_ref, wd_ref, gpost_ref, gfin_ref, o_ref, *, final, fc):
    xn = _rms(x, gpre_ref[...]).astype(BF16)
    d_ff = wg_ref.shape[1]
    acc = jnp.zeros(x.shape, F32)
    for c in range(0, d_ff, fc):
        g = _dot(xn, wg_ref[:, c:c + fc])
        u = _dot(xn, wu_ref[:, c:c + fc])
        a = (g * jax.nn.sigmoid(g)) * u
        acc = acc + _dot(a.astype(BF16), wd_ref[c:c + fc, :])
    h = x + 0.5 * _rms(acc, gpost_ref[...])
    if final:
        h = _rms(h, gfin_ref[...])
    o_ref[...] = h


def _ffn(x, g_pre, wg, wu, wd, g_post, g_fin, *, final, tm, mix=None):
    t, d = x.shape
    d_ff = wg.shape[1]
    fc = MXU_WIDTH
    assert d_ff % fc == 0
    const = lambda i: (0, 0)
    row = lambda i: (i, 0)
    if mix is None:
        body, lead, lead_specs = _ffn_kernel, (x,), [pl.BlockSpec((tm, d), row)]
    else:
        o_a, o_b, w_a, w_b, g_mix = mix
        body, lead = _ffn_after_mix_kernel, (o_a, o_b, x, w_a, w_b, g_mix)
        lead_specs = [pl.BlockSpec((tm, o_a.shape[1]), row), pl.BlockSpec((tm, o_b.shape[1]), row),
                      pl.BlockSpec((tm, d), row),
                      pl.BlockSpec(w_a.shape, const, pipeline_mode=pl.Buffered(1)),
                      pl.BlockSpec(w_b.shape, const, pipeline_mode=pl.Buffered(1)),
                      pl.BlockSpec((1, d), const)]
    return pl.pallas_call(
        functools.partial(body, final=final, fc=fc),
        grid=(t // tm,),
        in_specs=lead_specs + [
            pl.BlockSpec((1, d), const),
            pl.BlockSpec((d, d_ff), const, pipeline_mode=pl.Buffered(1)),
            pl.BlockSpec((d, d_ff), const, pipeline_mode=pl.Buffered(1)),
            pl.BlockSpec((d_ff, d), const, pipeline_mode=pl.Buffered(1)),
            pl.BlockSpec((1, d), const),
            pl.BlockSpec((1, d), const),
        ],
        out_specs=pl.BlockSpec((tm, d), lambda i: (i, 0)),
        out_shape=jax.ShapeDtypeStruct((t, d), F32),
        compiler_params=_params("parallel"),
        name="ffn_final" if final else "ffn",
    )(*lead, g_pre, wg, wu, wd, g_post, g_fin)


def _mix_in_kernel(h_ref, gpre_ref, win_ref, gq_ref, wuq_ref, wuk_ref, gkv_ref, cos_ref, sin_ref, sel_ref,
                   lat_ref, kr_ref, kn_ref, vn_ref, qcat_ref, kcat_ref, latt_ref, qsb_ref, ksb_ref, vsb_ref, ksq_ref,
                   *, heads, q_lora, kv_lora, rope_dim, sbw, mla_scale, sb_scale):
    u = _rms(h_ref[...], gpre_ref[...]).astype(BF16)
    proj = _dot(u, win_ref[...])
    cosp = cos_ref[...]
    sinp = sin_ref[...]
    o1 = q_lora
    o2 = o1 + kv_lora
    o3 = o2 + LANES

    def rope(grp):
        return grp * cosp + pltpu.roll(grp, LANES - rope_dim, 1) * sinp

    latent = _rms(proj[:, o1:o2], gkv_ref[...])
    krope = rope(proj[:, o2:o3])
    lat_ref[...] = latent
    kr_ref[...] = krope[:, :rope_dim]
    k_sb = proj[:, o3 + sbw:o3 + 2 * sbw]
    v_sb = proj[:, o3 + 2 * sbw:o3 + 3 * sbw]
    kn_ref[...] = k_sb
    vn_ref[...] = v_sb
    k_bf = k_sb.astype(BF16)
    ksb_ref[...] = k_bf
    vsb_ref[...] = v_sb.astype(BF16)
    qsb_ref[...] = (proj[:, o3:o3 + sbw] * sb_scale).astype(BF16)
    kcat_ref[...] = jnp.concatenate([latent, krope], axis=1).astype(BF16)
    latt_ref[...] = jnp.concatenate([latent.T, jnp.ones((ONES_ROWS, latent.shape[0]), F32)], axis=0).astype(BF16)
    k_f = k_bf.astype(F32)
    ksq_ref[...] = _dot((k_f * k_f).astype(BF16), sel_ref[...])

    c_q = _rms(proj[:, :o1], gq_ref[...]).astype(BF16)
    q = _dot(c_q, wuq_ref[...])
    n_nope = wuk_ref.shape[0]
    q_lat = _dot(q[:, :n_nope].astype(BF16), wuk_ref[...])
    for hd in range(heads):
        ql = q_lat[:, hd * kv_lora:(hd + 1) * kv_lora] * mla_scale
        qr = rope(q[:, n_nope + hd * LANES:n_nope + (hd + 1) * LANES]) * mla_scale
        qcat_ref[hd] = jnp.concatenate([ql, qr], axis=1).astype(BF16)


def _mix_in(h, g_pre, w_in_ext, g_q, w_uq_ext, w_uk_bd, g_kv, cosp, sinp, *, tm, dims):
    t, d = h.shape
    heads, q_lora, kv_lora, rope_dim, sbw, sb_heads, mla_scale, sb_scale = dims
    n_tab = cosp.shape[0] // tm
    const = lambda i: (0, 0)
    row = lambda i: (i, 0)
    tab = lambda i: (i % n_tab, 0)
    out_shape = (
        jax.ShapeDtypeStruct((t, kv_lora), F32),
        jax.ShapeDtypeStruct((t, rope_dim), F32),
        jax.ShapeDtypeStruct((t, sbw), F32),
        jax.ShapeDtypeStruct((t, sbw), F32),
        jax.ShapeDtypeStruct((heads, t, 2 * LANES), BF16),
        jax.ShapeDtypeStruct((t, 2 * LANES), BF16),
        jax.ShapeDtypeStruct((kv_lora + ONES_ROWS, t), BF16),
        jax.ShapeDtypeStruct((t, sbw), BF16),
        jax.ShapeDtypeStruct((t, sbw), BF16),
        jax.ShapeDtypeStruct((t, sbw), BF16),
        jax.ShapeDtypeStruct((t, LANES), F32),
    )
    out_specs = (
        pl.BlockSpec((tm, kv_lora), row),
        pl.BlockSpec((tm, rope_dim), row),
        pl.BlockSpec((tm, sbw), row),
        pl.BlockSpec((tm, sbw), row),
        pl.BlockSpec((heads, tm, 2 * LANES), lambda i: (0, i, 0)),
        pl.BlockSpec((tm, 2 * LANES), row),
        pl.BlockSpec((kv_lora + ONES_ROWS, tm), lambda i: (0, i)),
        pl.BlockSpec((tm, sbw), row),
        pl.BlockSpec((tm, sbw), row),
        pl.BlockSpec((tm, sbw), row),
        pl.BlockSpec((tm, LANES), row),
    )
    sel = (lax.broadcasted_iota(jnp.int32, (sbw, LANES), 0) // (sbw // sb_heads)
           == lax.broadcasted_iota(jnp.int32, (sbw, LANES), 1)).astype(BF16)
    return pl.pallas_call(
        functools.partial(_mix_in_kernel, heads=heads, q_lora=q_lora, kv_lora=kv_lora,
                          rope_dim=rope_dim, sbw=sbw, mla_scale=mla_scale, sb_scale=sb_scale),
        grid=(t // tm,),
        in_specs=[
            pl.BlockSpec((tm, d), row),
            pl.BlockSpec((1, d), const),
            pl.BlockSpec(w_in_ext.shape, const),
            pl.BlockSpec((1, q_lora), const),
            pl.BlockSpec(w_uq_ext.shape, const),
            pl.BlockSpec(w_uk_bd.shape, const),
            pl.BlockSpec((1, kv_lora), const),
            pl.BlockSpec((tm, LANES), tab),
            pl.BlockSpec((tm, LANES), tab),
            pl.BlockSpec((sbw, LANES), const),
        ],
        out_specs=out_specs,
        out_shape=out_shape,
        compiler_params=_params("parallel"),
        name="mix_in",
    )(h, g_pre, w_in_ext, g_q, w_uq_ext, w_uk_bd, g_kv, cosp, sinp, sel)


def _mla_scores(qs, ks):
    return sum(_dot_nt(kk, qq) for qq, kk in zip(qs, ks))


def _mla_probs(s, m_ref, visible):
    if visible is not None:
        s = jnp.where(visible, s, -jnp.inf)
    m_prev = m_ref[...]
    m_new = jnp.maximum(m_prev, jnp.max(s, axis=0, keepdims=True))
    m_ref[...] = m_new
    return jnp.exp2(m_prev - m_new), jnp.exp2(s - m_new)


def _mla_accumulate(alpha, p, v, acc_ref, v_transposed):
    pb = p.astype(BF16)
    if v_transposed:
        pv = _dot(v, pb)
    else:
        l = jnp.sum(p, axis=0, keepdims=True)
        pv = jnp.concatenate([_dot_tn(v, pb), jnp.broadcast_to(l, (ONES_ROWS, l.shape[1]))], axis=0)
    acc_ref[...] = alpha * acc_ref[...] + pv


def _mla_update(s, v, m_ref, acc_ref, visible, v_transposed):
    alpha, p = _mla_probs(s, m_ref, visible)
    _mla_accumulate(alpha, p, v, acc_ref, v_transposed)


def _own_block_rms_t(x, g_col, head_rows, tq):
    r = lax.broadcasted_iota(jnp.int32, x.shape, 0) // head_rows
    c = lax.broadcasted_iota(jnp.int32, x.shape, 1) // tq
    o = jnp.where(r == c, x, 0.0)
    inv = lax.rsqrt(jnp.sum(o * o, axis=0, keepdims=True) / head_rows + EPS)
    o_t = (o * inv * g_col).T
    out = o_t[:tq]
    for h in range(1, x.shape[1] // tq):
        out = out + o_t[h * tq:(h + 1) * tq]
    return out


def _mla_finish(acc_ref, wuvt_ref, g_ref, o_ref, heads, tq):
    c_dim = acc_ref.shape[0] - ONES_ROWS
    v_dim = wuvt_ref.shape[1]
    o_lat = (acc_ref[:c_dim, :] / acc_ref[c_dim:c_dim + 1, :]).astype(BF16)
    if tq % LANES == 0:
        outs = []
        for hd in range(heads):
            o = _dot(wuvt_ref[hd], o_lat[:, hd * tq:(hd + 1) * tq])
            outs.append(o * lax.rsqrt(jnp.mean(o * o, axis=0, keepdims=True) + EPS) * g_ref[hd])
        o_t = jnp.concatenate(outs, axis=0).T
    else:
        full = _dot(wuvt_ref[...].reshape(heads * v_dim, c_dim), o_lat)
        o_t = _own_block_rms_t(full, g_ref[...].reshape(heads * v_dim, 1), v_dim, tq)
    o_ref[...] = o_t.astype(BF16)


def _mla_init(m_ref, acc_ref):
    m_ref[...] = jnp.full(m_ref.shape, -jnp.inf, F32)
    acc_ref[...] = jnp.zeros(acc_ref.shape, F32)


def _mla_prompt_kernel(q_ref, k_ref, vt_ref, wuvt_ref, g_ref, o_ref, m_ref, acc_ref, *, heads, tq, tk):
    j = pl.program_id(1)
    q0 = j * tq
    q = q_ref[...].reshape(heads * tq, q_ref.shape[2])
    _mla_init(m_ref, acc_ref)
    n_full = q0 // tk

    def blocks(todo):
        lanes = [slice(r0, r0 + MLA_GROUP_LANES) for r0 in range(0, heads * tq, MLA_GROUP_LANES)]
        units = [(kb, vis, ln) for kb, vis in todo for ln in lanes]

        def rows(kb):
            return pl.ds(pl.multiple_of(kb * tk, tk), tk)

        s, ap = {}, {}
        lag_p, lag_a = MLA_SKEW
        for step in range(len(units) + lag_a):
            if step < len(units):
                kb, _, ln = units[step]
                s[step] = _dot_nt(k_ref[rows(kb), :], q[ln])
            u = step - lag_p
            if 0 <= u < len(units):
                kb, vis, ln = units[u]
                ap[u] = _mla_probs(s.pop(u), m_ref.at[:, ln], None if vis is None else vis[:, ln])
            u = step - lag_a
            if 0 <= u < len(units):
                kb, _, ln = units[u]
                _mla_accumulate(*ap.pop(u), vt_ref[:, rows(kb)], acc_ref.at[:, ln], True)

    def full_block(kb, carry):
        blocks([(kb, None)])
        return carry

    lax.fori_loop(0, n_full, full_block, 0)
    kpos = n_full * tk + lax.broadcasted_iota(jnp.int32, (tk, heads * tq), 0)
    qpos = q0 + lax.broadcasted_iota(jnp.int32, (tk, heads * tq), 1) % tq
    blocks([(n_full, (kpos // CHUNK) <= (qpos // CHUNK))])
    _mla_finish(acc_ref, wuvt_ref, g_ref, o_ref, heads, tq)


def _mla_scratch(rows, c_dim):
    return [pltpu.VMEM((1, rows), F32), pltpu.VMEM((c_dim + ONES_ROWS, rows), F32)]


def _mla_prompt(qcat, kcat, lat_t, w_uv_t, g_out, *, batch, seq, tq, tk):
    heads, t, dk = qcat.shape
    v_dim, c_dim = w_uv_t.shape[1], w_uv_t.shape[2]
    assert tk % tq == 0 and tq % CHUNK == 0 and seq % tk == 0 and tq % LANES == 0
    nq = seq // tq
    return pl.pallas_call(
        functools.partial(_mla_prompt_kernel, heads=heads, tq=tq, tk=tk),
        grid=(batch, nq),
        in_specs=[
            pl.BlockSpec((heads, tq, dk), lambda b, j: (0, b * nq + j, 0)),
            pl.BlockSpec((seq, dk), lambda b, j: (b, 0)),
            pl.BlockSpec((lat_t.shape[0], seq), lambda b, j: (0, b)),
            pl.BlockSpec(w_uv_t.shape, lambda b, j: (0, 0, 0)),
            pl.BlockSpec(g_out.shape, lambda b, j: (0, 0, 0)),
        ],
        out_specs=pl.BlockSpec((tq, heads * v_dim), lambda b, j: (b * nq + j, 0)),
        out_shape=jax.ShapeDtypeStruct((t, heads * v_dim), BF16),
        scratch_shapes=_mla_scratch(heads * tq, c_dim),
        compiler_params=_params("parallel", "arbitrary"),
        name="mla_prompt",
    )(qcat, kcat, lat_t, w_uv_t, g_out)


def _mla_sample_kernel(q_ref, lat_ref, kr_ref, knew_ref, wuvt_ref, g_ref, o_ref, m_ref, acc_ref,
                       *, heads, tq, tk, c_dim, rope_dim):
    q = q_ref[...].reshape(heads * tq, q_ref.shape[2])
    q_lat = q[:, :c_dim]
    q_rope = q[:, c_dim:c_dim + rope_dim]
    _mla_init(m_ref, acc_ref)
    n_blk = lat_ref.shape[0] // tk

    def latent(kb):
        return lat_ref[pl.ds(pl.multiple_of(kb * tk, tk), tk), :].astype(BF16)

    def scores(kb):
        kr = kr_ref[pl.ds(pl.multiple_of(kb * tk, tk), tk), :].astype(BF16)
        return _mla_scores([q_lat, q_rope], [latent(kb), kr])

    def past_block(kb, s):
        s_next = scores(kb + 1)
        _mla_update(s, latent(kb), m_ref, acc_ref, None, False)
        return s_next

    s = lax.fori_loop(0, n_blk - 1, past_block, scores(0))
    knew = knew_ref[...]
    s_new = _mla_scores([q], [knew])
    _mla_update(s, latent(n_blk - 1), m_ref, acc_ref, None, False)
    _mla_update(s_new, knew[:, :c_dim], m_ref, acc_ref, None, False)
    _mla_finish(acc_ref, wuvt_ref, g_ref, o_ref, heads, tq)


def _mla_sample(qcat, cache_lat, cache_kr, kcat, w_uv_t, g_out, *, tk):
    heads, t, dk = qcat.shape
    batch, n_past, c_dim = cache_lat.shape
    rope_dim = cache_kr.shape[2]
    tq = qh = t // batch
    v_dim = w_uv_t.shape[1]
    assert n_past % CHUNK == 0 and qh <= CHUNK and n_past % tk == 0 and (heads * tq) % LANES == 0
    return pl.pallas_call(
        functools.partial(_mla_sample_kernel, heads=heads, tq=tq, tk=tk, c_dim=c_dim, rope_dim=rope_dim),
        grid=(batch,),
        in_specs=[
            pl.BlockSpec((heads, qh, dk), lambda b: (0, b, 0)),
            pl.BlockSpec((None, n_past, c_dim), lambda b: (b, 0, 0)),
            pl.BlockSpec((None, n_past, rope_dim), lambda b: (b, 0, 0)),
            pl.BlockSpec((qh, dk), lambda b: (b, 0)),
            pl.BlockSpec(w_uv_t.shape, lambda b: (0, 0, 0)),
            pl.BlockSpec(g_out.shape, lambda b: (0, 0, 0)),
        ],
        out_specs=pl.BlockSpec((qh, heads * v_dim), lambda b: (b, 0)),
        out_shape=jax.ShapeDtypeStruct((t, heads * v_dim), BF16),
        scratch_shapes=_mla_scratch(heads * tq, c_dim),
        compiler_params=_params("parallel"),
        name="mla_sample",
    )(qcat, cache_lat, cache_kr, kcat, w_uv_t, g_out)


def _sb_update(q2, kblk, vblk, triu, carry, acc, before, skip_dead=False):
    z = _dot_nt(kblk, q2)
    sub = triu.shape[0]

    def live():
        sp = jnp.maximum(z, 0.0) + jnp.log(1.0 + jnp.exp(-jnp.abs(z)))
        if before is not None:
            sp = jnp.where(before, sp, 0.0)
        hi = sp.astype(BF16)
        lo = (sp - hi.astype(F32)).astype(BF16)
        cs, run = [], carry
        for r0 in reversed(range(0, z.shape[0], sub)):
            cs.insert(0, _dot(triu, hi[r0:r0 + sub]) + _dot(triu, lo[r0:r0 + sub]) + run)
            run = cs[0][0:1, :]
        c = cs[0] if len(cs) == 1 else jnp.concatenate(cs, axis=0)
        a = jnp.exp(z - c)
        if before is not None:
            a = jnp.where(before, a, 0.0)
        return run, acc + _dot_tn(vblk() if callable(vblk) else vblk, a.astype(BF16))

    if not skip_dead:
        return live()
    return lax.cond(jnp.max(z - carry) > DEAD_EXPONENT, live, lambda: (carry, acc))


def _sb_queries(q, head_dim):
    lane = lax.broadcasted_iota(jnp.int32, (1, q.shape[1]), 1) // head_dim
    return jnp.concatenate([jnp.where(lane == h, q, jnp.zeros_like(q)) for h in range(q.shape[1] // head_dim)], axis=0)


def _sb_query_norms(q2):
    q2f = q2.astype(F32)
    return jnp.sqrt(_dot_nt(jnp.ones((SUBLANES, q2.shape[1]), BF16), (q2f * q2f).astype(BF16))[0:1, :]) * NORM_MARGIN


def _sb_before(tk, tq, n, first_query_row=0):
    key = lax.broadcasted_iota(jnp.int32, (tk, n), 0) - first_query_row
    qry = lax.broadcasted_iota(jnp.int32, (tk, n), 1) % tq
    return key < qry


def _sb_finish(acc, g_ref, o_ref, head_dim, tq):
    o_ref[...] = _own_block_rms_t(acc, g_ref[...], head_dim, tq).astype(BF16)


def _sb_prompt_kernel(kmax_ref, q_ref, k_ref, v_ref, tri_ref, g_ref, o_ref, *, tq, head_dim):
    b, p, j = pl.program_id(0), pl.program_id(1), pl.program_id(2)
    n_blk = pl.num_programs(2)
    q2 = _sb_queries(q_ref[...], head_dim)
    width = q_ref.shape[1]
    n = q2.shape[0]
    hpg = width // head_dim
    triu = tri_ref[...]
    zero = (jnp.zeros((1, n), F32), jnp.zeros((width, n), F32))

    def own_block():
        return _sb_update(q2, k_ref[pl.ds(0, tq), :], v_ref[pl.ds(0, tq), :], triu, *zero, _sb_before(tq, tq, n))

    def own_and_previous_block():
        k0 = pl.multiple_of((j - 1) * tq, tq)
        return _sb_update(q2, k_ref[pl.ds(k0, 2 * tq), :], v_ref[pl.ds(k0, 2 * tq), :], triu, *zero,
                          _sb_before(2 * tq, tq, n, first_query_row=tq))

    carry, acc = lax.cond(j > 0, own_and_previous_block, own_block)
    qn = _sb_query_norms(q2)
    lane_head = lax.broadcasted_iota(jnp.int32, (1, n), 1) // tq

    def alive(kb, carry):
        base = (b * n_blk + jnp.maximum(kb, 0)) * (hpg * pl.num_programs(1)) + p * hpg
        kmax = jnp.zeros((1, n), F32)
        for h in range(hpg):
            kmax = jnp.where(lane_head == h, kmax_ref[base + h], kmax)
        return jnp.logical_and(kb >= 0, jnp.max(qn * kmax - carry) > DEAD_EXPONENT)

    def cond(state):
        return state[1]

    def earlier_block(state):
        kb, _, carry, acc = state
        k0 = pl.multiple_of(kb * tq, tq)
        carry, acc = _sb_update(q2, k_ref[pl.ds(k0, tq), :], v_ref[pl.ds(k0, tq), :], triu, carry, acc, None)
        return kb - 1, alive(kb - 1, carry), carry, acc

    state = lax.while_loop(cond, earlier_block, (j - 2, alive(j - 2, carry), carry, acc))
    _sb_finish(state[3], g_ref, o_ref, head_dim, tq)


def _triu(n):
    r = lax.broadcasted_iota(jnp.int32, (n, n), 0)
    c = lax.broadcasted_iota(jnp.int32, (n, n), 1)
    return (c >= r).astype(BF16)


def _sb_prompt(q_sb, k_sb, v_sb, ksq, g_col, *, batch, seq, tq, head_dim, gw):
    t, sbw = q_sb.shape
    nq = seq // tq
    n_grp = sbw // gw
    n_heads = sbw // head_dim
    blk_max = jnp.max(ksq[:, :n_heads].reshape(batch, nq, tq, n_heads), axis=2)
    kmax = (jnp.sqrt(lax.cummax(blk_max, axis=1)) * NORM_MARGIN).reshape(-1)
    return pl.pallas_call(
        functools.partial(_sb_prompt_kernel, tq=tq, head_dim=head_dim),
        grid_spec=pltpu.PrefetchScalarGridSpec(
            num_scalar_prefetch=1,
            grid=(batch, n_grp, nq),
            in_specs=[
                pl.BlockSpec((tq, gw), lambda b, p, j, kmax: (b * nq + j, p)),
                pl.BlockSpec((seq, gw), lambda b, p, j, kmax: (b, p)),
                pl.BlockSpec((seq, gw), lambda b, p, j, kmax: (b, p)),
                pl.BlockSpec((tq, tq), lambda b, p, j, kmax: (0, 0)),
                pl.BlockSpec((gw, 1), lambda b, p, j, kmax: (p, 0)),
            ],
            out_specs=pl.BlockSpec((tq, gw), lambda b, p, j, kmax: (b * nq + j, p)),
        ),
        out_shape=jax.ShapeDtypeStruct((t, sbw), BF16),
        compiler_params=_params("parallel", "parallel", "arbitrary"),
        name="sb_prompt",
    )(kmax, q_sb, k_sb, v_sb, _triu(tq), g_col)


def _sb_sample_kernel(q_ref, kc_ref, vc_ref, kn_ref, vn_ref, tri_ref, g_ref, o_ref, *, tk, head_dim):
    tq, width = q_ref.shape
    q2 = _sb_queries(q_ref[...], head_dim)
    n = q2.shape[0]
    r = lax.broadcasted_iota(jnp.int32, (tq, tq), 0)
    c = lax.broadcasted_iota(jnp.int32, (tq, tq), 1)
    state = _sb_update(q2, kn_ref[...], vn_ref[...], (c >= r).astype(BF16),
                       jnp.zeros((1, n), F32), jnp.zeros((width, n), F32), _sb_before(tq, tq, n))
    triu = tri_ref[...]
    n_blk = kc_ref.shape[0] // tk

    def cached_block(kb, state, skip_dead):
        k0 = pl.multiple_of(kb * tk, tk)
        return _sb_update(q2, kc_ref[pl.ds(k0, tk), :].astype(BF16), lambda: vc_ref[pl.ds(k0, tk), :].astype(BF16),
                          triu, state[0], state[1], None, skip_dead=skip_dead)

    state = cached_block(n_blk - 1, state, False)
    rest = (n_blk - 1) * tk
    z_rest = _dot_nt(kc_ref[:rest, :].astype(BF16), q2)

    def walk():
        return lax.fori_loop(0, n_blk - 1, lambda i, st: cached_block(n_blk - 2 - i, st, True), state)

    state = lax.cond(jnp.max(z_rest - state[0]) > DEAD_EXPONENT, walk, lambda: state)
    _sb_finish(state[1], g_ref, o_ref, head_dim, tq)


def _sb_sample_full(q_sb, cache_k, cache_v, k_sb, v_sb, g_col, *, tk, head_dim):
    t, sbw = q_sb.shape
    batch, n_past, _ = cache_k.shape
    tq = t // batch
    n_grp = sbw // LANES
    assert n_past % tk == 0
    new = pl.BlockSpec((tq, LANES), lambda b, p: (b, p))
    past = pl.BlockSpec((None, n_past, LANES), lambda b, p: (b, 0, p))
    return pl.pallas_call(
        functools.partial(_sb_sample_kernel, tk=tk, head_dim=head_dim),
        grid=(batch, n_grp),
        in_specs=[new, past, past, new, new,
                  pl.BlockSpec((tk, tk), lambda b, p: (0, 0)),
                  pl.BlockSpec((LANES, 1), lambda b, p: (p, 0))],
        out_specs=new,
        out_shape=jax.ShapeDtypeStruct((t, sbw), BF16),
        compiler_params=_params("parallel", "parallel"),
        name="sb_sample_full",
    )(q_sb, cache_k, cache_v, k_sb, v_sb, _triu(tk), g_col)


def _sb_sample_recent_kernel(q_ref, kc_ref, vl_ref, kn_ref, vn_ref, tri_ref, g_ref, o_ref, bound_ref, *, head_dim):
    tq, width = q_ref.shape
    tk = vl_ref.shape[0]
    rest = kc_ref.shape[0] - tk
    q2 = _sb_queries(q_ref[...], head_dim)
    n = q2.shape[0]
    r = lax.broadcasted_iota(jnp.int32, (tq, tq), 0)
    c = lax.broadcasted_iota(jnp.int32, (tq, tq), 1)
    state = _sb_update(q2, kn_ref[...], vn_ref[...], (c >= r).astype(BF16),
                       jnp.zeros((1, n), F32), jnp.zeros((width, n), F32), _sb_before(tq, tq, n))
    carry, acc = _sb_update(q2, kc_ref[rest:, :].astype(BF16), vl_ref[...].astype(BF16), tri_ref[...], *state, None)
    z_rest = _dot_nt(kc_ref[:rest, :].astype(BF16), q2)
    bound = jnp.max(jnp.max(z_rest - carry, axis=0, keepdims=True), axis=1, keepdims=True)
    bound_ref[...] = jnp.broadcast_to(bound, bound_ref.shape)
    _sb_finish(acc, g_ref, o_ref, head_dim, tq)


def _sb_sample(q_sb, cache_k, cache_v, k_sb, v_sb, g_col, *, tk, head_dim):
    t, sbw = q_sb.shape
    batch, n_past = cache_k.shape[:2]
    tq = t // batch
    n_grp = sbw // LANES
    assert n_past % tk == 0
    keys = cache_k.reshape(batch, n_past, sbw)
    v_last = cache_v[:, n_past - tk:].reshape(batch, tk, sbw)
    new = pl.BlockSpec((tq, LANES), lambda b, p: (b, p))
    o_fast, bound = pl.pallas_call(
        functools.partial(_sb_sample_recent_kernel, head_dim=head_dim),
        grid=(batch, n_grp),
        in_specs=[new,
                  pl.BlockSpec((None, n_past, LANES), lambda b, p: (b, 0, p)),
                  pl.BlockSpec((None, tk, LANES), lambda b, p: (b, 0, p)),
                  new, new,
                  pl.BlockSpec((tk, tk), lambda b, p: (0, 0)),
                  pl.BlockSpec((LANES, 1), lambda b, p: (p, 0))],
        out_specs=(new, pl.BlockSpec((None, None, SUBLANES, LANES), lambda b, p: (b, p, 0, 0))),
        out_shape=(jax.ShapeDtypeStruct((t, sbw), BF16),
                   jax.ShapeDtypeStruct((batch, n_grp, SUBLANES, LANES), F32)),
        compiler_params=_params("parallel", "parallel"),
        name="sb_sample",
    )(q_sb, keys, v_last, k_sb, v_sb, _triu(tk), g_col)

    def full():
        return _sb_sample_full(q_sb, keys, cache_v.reshape(batch, n_past, sbw), k_sb, v_sb, g_col,
                               tk=tk, head_dim=head_dim)

    return lax.cond(jnp.max(bound) > DEAD_EXPONENT, full, lambda: o_fast)


def _rot_cols(w):
    half = w.shape[-1] // 2
    return jnp.concatenate([-w[..., half:], w[..., :half]], axis=-1)


def _rope_tables(pos, rope_dim, rows):
    half = rope_dim // 2
    inv_freq = ROPE_THETA ** (-jnp.arange(half, dtype=F32) / half)
    ang = pos.astype(F32)[:, None] * inv_freq[None, :]
    pad = jnp.zeros((pos.shape[0], LANES - rope_dim), F32)
    cosp = jnp.concatenate([jnp.cos(ang), jnp.cos(ang), pad], axis=1)
    sinp = jnp.concatenate([jnp.sin(ang), jnp.sin(ang), pad], axis=1)
    reps = max(1, rows // pos.shape[0])
    return jnp.tile(cosp, (reps, 1)), jnp.tile(sinp, (reps, 1))


def kernel(x_prompt, x_sample, cache_mla_latent, cache_mla_krope, cache_sb_k, cache_sb_v, g_pre_ff1, w_gate1, w_up1, w_down1, g_post_ff1, g_pre_mix, w_in, g_q, w_uq, g_kv, w_uk, w_uv, g_mla_out, g_sb_out, w_out, g_post_mix, g_pre_ff2, w_gate2, w_up2, w_down2, g_post_ff2, g_final):
    depth = w_in.shape[0]
    assert depth == 1
    batch, seq, d = x_prompt.shape
    dec_batch, dec_seq, _ = x_sample.shape
    n_past = cache_mla_latent.shape[2]
    _, q_lora, heads, qk_dim = w_uq.shape
    kv_lora, _, nope = w_uk.shape[1:]
    rope_dim = qk_dim - nope
    v_dim = w_uv.shape[3]
    sb_heads, sb_dim = g_sb_out.shape[1:]
    sbw = sb_heads * sb_dim
    mla_scale = float(qk_dim) ** -0.5
    sb_scale = float(sb_dim) ** -0.5
    assert kv_lora == LANES and 2 * rope_dim <= LANES and LANES % sb_dim == 0
    dims = (heads, q_lora, kv_lora, rope_dim, sbw, sb_heads, mla_scale * LOG2E, sb_scale)

    row = lambda g: g.reshape(1, -1).astype(F32)
    l = 0
    wi = w_in[l]
    i1 = q_lora + kv_lora
    i2 = i1 + rope_dim
    w_kr = wi[:, i1:i2]
    w_in_ext = jnp.concatenate(
        [wi[:, :i1], w_kr, _rot_cols(w_kr), jnp.zeros((d, LANES - 2 * rope_dim), F32), wi[:, i2:]], axis=1).astype(BF16)
    wq = w_uq[l]
    wq_rope = wq[:, :, nope:]
    wq_grp = jnp.concatenate(
        [wq_rope, _rot_cols(wq_rope), jnp.zeros((q_lora, heads, LANES - 2 * rope_dim), F32)], axis=2)
    w_uq_ext = jnp.concatenate(
        [wq[:, :, :nope].reshape(q_lora, heads * nope), wq_grp.reshape(q_lora, heads * LANES)], axis=1).astype(BF16)
    wk = jnp.transpose(w_uk[l], (1, 2, 0))
    eye = jnp.eye(heads, dtype=F32)
    w_uk_bd = (wk[:, :, None, :] * eye[:, None, :, None]).reshape(heads * nope, heads * kv_lora).astype(BF16)
    w_uv_t = jnp.transpose(w_uv[l], (1, 2, 0)).astype(BF16)
    w_out_a = w_out[l][:heads * v_dim].astype(BF16)
    w_out_b = w_out[l][heads * v_dim:].astype(BF16)
    ffn1 = (row(g_pre_ff1[l]), w_gate1[l].astype(BF16), w_up1[l].astype(BF16), w_down1[l].astype(BF16),
            row(g_post_ff1[l]), row(g_final[l]))
    ffn2 = (row(g_pre_ff2[l]), w_gate2[l].astype(BF16), w_up2[l].astype(BF16), w_down2[l].astype(BF16),
            row(g_post_ff2[l]), row(g_final[l]))
    g_mla = g_mla_out[l].reshape(heads, v_dim, 1).astype(F32)
    g_sb = g_sb_out[l].reshape(sbw, 1).astype(F32)

    def layer(x, pos, tm, attend):
        h = _ffn(x, *ffn1, final=False, tm=tm)
        cosp, sinp = _rope_tables(pos, rope_dim, tm)
        lat, kr, kn, vn, qcat, kcat, lat_t, q_sb, k_sb, v_sb, ksq = _mix_in(
            h, row(g_pre_mix[l]), w_in_ext, row(g_q[l]), w_uq_ext, w_uk_bd, row(g_kv[l]), cosp, sinp,
            tm=tm, dims=dims)
        o_mla, o_sb = attend(qcat, kcat, lat_t, q_sb, k_sb, v_sb, ksq)
        y = _ffn(h, *ffn2, final=True, tm=tm, mix=(o_mla, o_sb, w_out_a, w_out_b, row(g_post_mix[l])))
        return y, (lat, kr, kn, vn)

    def attend_prompt(qcat, kcat, lat_t, q_sb, k_sb, v_sb, ksq):
        o_mla = _mla_prompt(qcat, kcat, lat_t, w_uv_t, g_mla, batch=batch, seq=seq, tq=512, tk=512)
        o_sb = _sb_prompt(q_sb, k_sb, v_sb, ksq, g_sb, batch=batch, seq=seq, tq=256, head_dim=sb_dim, gw=2 * LANES)
        return o_mla, o_sb

    def attend_sample(qcat, kcat, lat_t, q_sb, k_sb, v_sb, ksq):
        o_mla = _mla_sample(qcat, cache_mla_latent[l], cache_mla_krope[l], kcat, w_uv_t, g_mla, tk=512)
        o_sb = _sb_sample(q_sb, cache_sb_k[l], cache_sb_v[l], k_sb, v_sb, g_sb, tk=256, head_dim=sb_dim)
        return o_mla, o_sb

    pos_p = jnp.arange(seq, dtype=jnp.int32)
    pos_s = n_past + jnp.arange(dec_seq, dtype=jnp.int32)
    yp, rp = layer(x_prompt.reshape(batch * seq, d), pos_p, 512, attend_prompt)
    ys, rs = layer(x_sample.reshape(dec_batch * dec_seq, d), pos_s, 512, attend_sample)

    def rows(r, b, s):
        lat, kr, kn, vn = r
        return (lat.reshape(1, b, s, kv_lora), kr.reshape(1, b, s, rope_dim),
                kn.reshape(1, b, s, sb_heads, sb_dim), vn.reshape(1, b, s, sb_heads, sb_dim))

    return (yp.reshape(batch, seq, d), ys.reshape(dec_batch, dec_seq, d)) + rows(rp, batch, seq) + rows(rs, dec_batch, dec_seq)
```

```python
import functools

import jax
import jax.numpy as jnp
from jax import lax
from jax.experimental import pallas as pl
from jax.experimental.pallas import tpu as pltpu

EPS = 1e-6
CHUNK = 64
ROPE_THETA = 10000.0
LANES = 128
F32 = jnp.float32
BF16 = jnp.bfloat16
VMEM_LIMIT = 56 * 1024 * 1024
SUBLANES = 8
MXU_WIDTH = 256
ONES_ROWS = 2 * SUBLANES
LOG2E = 1.4426950408889634
DEAD_EXPONENT = -120.0
NORM_MARGIN = 1.02
MLA_SKEW = (2, 3)
MLA_GROUP_LANES = 512


def _dot(a, b):
    return jnp.dot(a, b, preferred_element_type=F32)


def _dot_nt(a, b):
    return lax.dot_general(a, b, (((1,), (1,)), ((), ())), preferred_element_type=F32)


def _dot_tn(a, b):
    return lax.dot_general(a, b, (((0,), (0,)), ((), ())), preferred_element_type=F32)


def _rms(x, g):
    return x * lax.rsqrt(jnp.mean(x * x, axis=-1, keepdims=True) + EPS) * g


def _params(*sem):
    return pltpu.CompilerParams(dimension_semantics=sem, vmem_limit_bytes=VMEM_LIMIT)


def _ffn_kernel(x_ref, *rest, final, fc):
    _ffn_body(x_ref[...], *rest, final=final, fc=fc)


def _ffn_after_mix_kernel(oa_ref, ob_ref, h_ref, wa_ref, wb_ref, gmix_ref, *rest, final, fc):
    m = _dot(oa_ref[...], wa_ref[...]) + _dot(ob_ref[...], wb_ref[...])
    _ffn_body(h_ref[...] + _rms(m, gmix_ref[...]), *rest, final=final, fc=fc)


def _ffn_body(x, gpre_ref, wg_ref, wu_ref, wd_ref, gpost_ref, gfin_ref, o_ref, *, final, fc):
    xn = _rms(x, gpre_ref[...]).astype(BF16)
    d_ff = wg_ref.shape[1]
    acc = jnp.zeros(x.shape, F32)
    for c in range(0, d_ff, fc):
        g = _dot(xn, wg_ref[:, c:c + fc])
        u = _dot(xn, wu_ref[:, c:c + fc])
        a = (g * jax.nn.sigmoid(g)) * u
        acc = acc + _dot(a.astype(BF16), wd_ref[c:c + fc, :])
    h = x + 0.5 * _rms(acc, gpost_ref[...])
    if final:
        h = _rms(h, gfin_ref[...])
    o_ref[...] = h


def _ffn(x, g_pre, wg, wu, wd, g_post, g_fin, *, final, tm, mix=None):
    t, d = x.shape
    d_ff = wg.shape[1]
    fc = MXU_WIDTH
    assert d_ff % fc == 0
    const = lambda i: (0, 0)
    row = lambda i: (i, 0)
    if mix is None:
        body, lead, lead_specs = _ffn_kernel, (x,), [pl.BlockSpec((tm, d), row)]
    else:
        o_a, o_b, w_a, w_b, g_mix = mix
        body, lead = _ffn_after_mix_kernel, (o_a, o_b, x, w_a, w_b, g_mix)
        lead_specs = [pl.BlockSpec((tm, o_a.shape[1]), row), pl.BlockSpec((tm, o_b.shape[1]), row),
                      pl.BlockSpec((tm, d), row),
                      pl.BlockSpec(w_a.shape, const, pipeline_mode=pl.Buffered(1)),
                      pl.BlockSpec(w_b.shape, const, pipeline_mode=pl.Buffered(1)),
                      pl.BlockSpec((1, d), const)]
    return pl.pallas_call(
        functools.partial(body, final=final, fc=fc),
        grid=(t // tm,),
        in_specs=lead_specs + [
            pl.BlockSpec((1, d), const),
            pl.BlockSpec((d, d_ff), const, pipeline_mode=pl.Buffered(1)),
            pl.BlockSpec((d, d_ff), const, pipeline_mode=pl.Buffered(1)),
            pl.BlockSpec((d_ff, d), const, pipeline_mode=pl.Buffered(1)),
            pl.BlockSpec((1, d), const),
            pl.BlockSpec((1, d), const),
        ],
        out_specs=pl.BlockSpec((tm, d), lambda i: (i, 0)),
        out_shape=jax.ShapeDtypeStruct((t, d), F32),
        compiler_params=_params("parallel"),
        name="ffn_final" if final else "ffn",
    )(*lead, g_pre, wg, wu, wd, g_post, g_fin)


def _mix_in_kernel(h_ref, gpre_ref, win_ref, gq_ref, wuq_ref, wuk_ref, gkv_ref, cos_ref, sin_ref, sel_ref,
                   lat_ref, kr_ref, kn_ref, vn_ref, qcat_ref, kcat_ref, latt_ref, qsb_ref, ksb_ref, vsb_ref, ksq_ref,
                   *, heads, q_lora, kv_lora, rope_dim, sbw, mla_scale, sb_scale):
    u = _rms(h_ref[...], gpre_ref[...]).astype(BF16)
    proj = _dot(u, win_ref[...])
    cosp = cos_ref[...]
    sinp = sin_ref[...]
    o1 = q_lora
    o2 = o1 + kv_lora
    o3 = o2 + LANES

    def rope(grp):
        return grp * cosp + pltpu.roll(grp, LANES - rope_dim, 1) * sinp

    latent = _rms(proj[:, o1:o2], gkv_ref[...])
    krope = rope(proj[:, o2:o3])
    lat_ref[...] = latent
    kr_ref[...] = krope[:, :rope_dim]
    k_sb = proj[:, o3 + sbw:o3 + 2 * sbw]
    v_sb = proj[:, o3 + 2 * sbw:o3 + 3 * sbw]
    kn_ref[...] = k_sb
    vn_ref[...] = v_sb
    k_bf = k_sb.astype(BF16)
    ksb_ref[...] = k_bf
    vsb_ref[...] = v_sb.astype(BF16)
    qsb_ref[...] = (proj[:, o3:o3 + sbw] * sb_scale).astype(BF16)
    kcat_ref[...] = jnp.concatenate([latent, krope], axis=1).astype(BF16)
    latt_ref[...] = jnp.concatenate([latent.T, jnp.ones((ONES_ROWS, latent.shape[0]), F32)], axis=0).astype(BF16)
    k_f = k_bf.astype(F32)
    ksq_ref[...] = _dot((k_f * k_f).astype(BF16), sel_ref[...])

    c_q = _rms(proj[:, :o1], gq_ref[...]).astype(BF16)
    q = _dot(c_q, wuq_ref[...])
    n_nope = wuk_ref.shape[0]
    q_lat = _dot(q[:, :n_nope].astype(BF16), wuk_ref[...])
    for hd in range(heads):
        ql = q_lat[:, hd * kv_lora:(hd + 1) * kv_lora] * mla_scale
        qr = rope(q[:, n_nope + hd * LANES:n_nope + (hd + 1) * LANES]) * mla_scale
        qcat_ref[hd] = jnp.concatenate([ql, qr], axis=1).astype(BF16)


def _mix_in(h, g_pre, w_in_ext, g_q, w_uq_ext, w_uk_bd, g_kv, cosp, sinp, *, tm, dims):
    t, d = h.shape
    heads, q_lora, kv_lora, rope_dim, sbw, sb_heads, mla_scale, sb_scale = dims
    n_tab = cosp.shape[0] // tm
    const = lambda i: (0, 0)
    row = lambda i: (i, 0)
    tab = lambda i: (i % n_tab, 0)
    out_shape = (
        jax.ShapeDtypeStruct((t, kv_lora), F32),
        jax.ShapeDtypeStruct((t, rope_dim), F32),
        jax.ShapeDtypeStruct((t, sbw), F32),
        jax.ShapeDtypeStruct((t, sbw), F32),
        jax.ShapeDtypeStruct((heads, t, 2 * LANES), BF16),
        jax.ShapeDtypeStruct((t, 2 * LANES), BF16),
        jax.ShapeDtypeStruct((kv_lora + ONES_ROWS, t), BF16),
        jax.ShapeDtypeStruct((t, sbw), BF16),
        jax.ShapeDtypeStruct((t, sbw), BF16),
        jax.ShapeDtypeStruct((t, sbw), BF16),
        jax.ShapeDtypeStruct((t, LANES), F32),
    )
    out_specs = (
        pl.BlockSpec((tm, kv_lora), row),
        pl.BlockSpec((tm, rope_dim), row),
        pl.BlockSpec((tm, sbw), row),
        pl.BlockSpec((tm, sbw), row),
        pl.BlockSpec((heads, tm, 2 * LANES), lambda i: (0, i, 0)),
        pl.BlockSpec((tm, 2 * LANES), row),
        pl.BlockSpec((kv_lora + ONES_ROWS, tm), lambda i: (0, i)),
        pl.BlockSpec((tm, sbw), row),
        pl.BlockSpec((tm, sbw), row),
        pl.BlockSpec((tm, sbw), row),
        pl.BlockSpec((tm, LANES), row),
    )
    sel = (lax.broadcasted_iota(jnp.int32, (sbw, LANES), 0) // (sbw // sb_heads)
           == lax.broadcasted_iota(jnp.int32, (sbw, LANES), 1)).astype(BF16)
    return pl.pallas_call(
        functools.partial(_mix_in_kernel, heads=heads, q_lora=q_lora, kv_lora=kv_lora,
                          rope_dim=rope_dim, sbw=sbw, mla_scale=mla_scale, sb_scale=sb_scale),
        grid=(t // tm,),
        in_specs=[
            pl.BlockSpec((tm, d), row),
            pl.BlockSpec((1, d), const),
            pl.BlockSpec(w_in_ext.shape, const),
            pl.BlockSpec((1, q_lora), const),
            pl.BlockSpec(w_uq_ext.shape, const),
            pl.BlockSpec(w_uk_bd.shape, const),
            pl.BlockSpec((1, kv_lora), const),
            pl.BlockSpec((tm, LANES), tab),
            pl.BlockSpec((tm, LANES), tab),
            pl.BlockSpec((sbw, LANES), const),
        ],
        out_specs=out_specs,
        out_shape=out_shape,
        compiler_params=_params("parallel"),
        name="mix_in",
    )(h, g_pre, w_in_ext, g_q, w_uq_ext, w_uk_bd, g_kv, cosp, sinp, sel)


def _mla_scores(qs, ks):
    return sum(_dot_nt(kk, qq) for qq, kk in zip(qs, ks))


def _mla_probs(s, m_ref, visible):
    if visible is not None:
        s = jnp.where(visible, s, -jnp.inf)
    m_prev = m_ref[...]
    m_new = jnp.maximum(m_prev, jnp.max(s, axis=0, keepdims=True))
    m_ref[...] = m_new
    return jnp.exp2(m_prev - m_new), jnp.exp2(s - m_new)


def _mla_accumulate(alpha, p, v, acc_ref, v_transposed):
    pb = p.astype(BF16)
    if v_transposed:
        pv = _dot(v, pb)
    else:
        l = jnp.sum(p, axis=0, keepdims=True)
        pv = jnp.concatenate([_dot_tn(v, pb), jnp.broadcast_to(l, (ONES_ROWS, l.shape[1]))], axis=0)
    acc_ref[...] = alpha * acc_ref[...] + pv


def _mla_update(s, v, m_ref, acc_ref, visible, v_transposed):
    alpha, p = _mla_probs(s, m_ref, visible)
    _mla_accumulate(alpha, p, v, acc_ref, v_transposed)


def _own_block_rms_t(x, g_col, head_rows, tq):
    r = lax.broadcasted_iota(jnp.int32, x.shape, 0) // head_rows
    c = lax.broadcasted_iota(jnp.int32, x.shape, 1) // tq
    o = jnp.where(r == c, x, 0.0)
    inv = lax.rsqrt(jnp.sum(o * o, axis=0, keepdims=True) / head_rows + EPS)
    o_t = (o * inv * g_col).T
    out = o_t[:tq]
    for h in range(1, x.shape[1] // tq):
        out = out + o_t[h * tq:(h + 1) * tq]
    return out


def _mla_finish(acc_ref, wuvt_ref, g_ref, o_ref, heads, tq):
    c_dim = acc_ref.shape[0] - ONES_ROWS
    v_dim = wuvt_ref.shape[1]
    o_lat = (acc_ref[:c_dim, :] / acc_ref[c_dim:c_dim + 1, :]).astype(BF16)
    if tq % LANES == 0:
        outs = []
        for hd in range(heads):
            o = _dot(wuvt_ref[hd], o_lat[:, hd * tq:(hd + 1) * tq])
            outs.append(o * lax.rsqrt(jnp.mean(o * o, axis=0, keepdims=True) + EPS) * g_ref[hd])
        o_t = jnp.concatenate(outs, axis=0).T
    else:
        full = _dot(wuvt_ref[...].reshape(heads * v_dim, c_dim), o_lat)
        o_t = _own_block_rms_t(full, g_ref[...].reshape(heads * v_dim, 1), v_dim, tq)
    o_ref[...] = o_t.astype(BF16)


def _mla_init(m_ref, acc_ref):
    m_ref[...] = jnp.full(m_ref.shape, -jnp.inf, F32)
    acc_ref[...] = jnp.zeros(acc_ref.shape, F32)


def _mla_prompt_kernel(q_ref, k_ref, vt_ref, wuvt_ref, g_ref, o_ref, m_ref, acc_ref, *, heads, tq, tk):
    j = pl.program_id(1)
    q0 = j * tq
    q = q_ref[...].reshape(heads * tq, q_ref.shape[2])
    _mla_init(m_ref, acc_ref)
    n_full = q0 // tk

    def blocks(todo):
        lanes = [slice(r0, r0 + MLA_GROUP_LANES) for r0 in range(0, heads * tq, MLA_GROUP_LANES)]
        units = [(kb, vis, ln) for kb, vis in todo for ln in lanes]

        def rows(kb):
            return pl.ds(pl.multiple_of(kb * tk, tk), tk)

        s, ap = {}, {}
        lag_p, lag_a = MLA_SKEW
        for step in range(len(units) + lag_a):
            if step < len(units):
                kb, _, ln = units[step]
                s[step] = _dot_nt(k_ref[rows(kb), :], q[ln])
            u = step - lag_p
            if 0 <= u < len(units):
                kb, vis, ln = units[u]
                ap[u] = _mla_probs(s.pop(u), m_ref.at[:, ln], None if vis is None else vis[:, ln])
            u = step - lag_a
            if 0 <= u < len(units):
                kb, _, ln = units[u]
                _mla_accumulate(*ap.pop(u), vt_ref[:, rows(kb)], acc_ref.at[:, ln], True)

    def full_block(kb, carry):
        blocks([(kb, None)])
        return carry

    lax.fori_loop(0, n_full, full_block, 0)
    kpos = n_full * tk + lax.broadcasted_iota(jnp.int32, (tk, heads * tq), 0)
    qpos = q0 + lax.broadcasted_iota(jnp.int32, (tk, heads * tq), 1) % tq
    blocks([(n_full, (kpos // CHUNK) <= (qpos // CHUNK))])
    _mla_finish(acc_ref, wuvt_ref, g_ref, o_ref, heads, tq)


def _mla_scratch(rows, c_dim):
    return [pltpu.VMEM((1, rows), F32), pltpu.VMEM((c_dim + ONES_ROWS, rows), F32)]


def _mla_prompt(qcat, kcat, lat_t, w_uv_t, g_out, *, batch, seq, tq, tk):
    heads, t, dk = qcat.shape
    v_dim, c_dim = w_uv_t.shape[1], w_uv_t.shape[2]
    assert tk % tq == 0 and tq % CHUNK == 0 and seq % tk == 0 and tq % LANES == 0
    nq = seq // tq
    return pl.pallas_call(
        functools.partial(_mla_prompt_kernel, heads=heads, tq=tq, tk=tk),
        grid=(batch, nq),
        in_specs=[
            pl.BlockSpec((heads, tq, dk), lambda b, j: (0, b * nq + j, 0)),
            pl.BlockSpec((seq, dk), lambda b, j: (b, 0)),
            pl.BlockSpec((lat_t.shape[0], seq), lambda b, j: (0, b)),
            pl.BlockSpec(w_uv_t.shape, lambda b, j: (0, 0, 0)),
            pl.BlockSpec(g_out.shape, lambda b, j: (0, 0, 0)),
        ],
        out_specs=pl.BlockSpec((tq, heads * v_dim), lambda b, j: (b * nq + j, 0)),
        out_shape=jax.ShapeDtypeStruct((t, heads * v_dim), BF16),
        scratch_shapes=_mla_scratch(heads * tq, c_dim),
        compiler_params=_params("parallel", "arbitrary"),
        name="mla_prompt",
    )(qcat, kcat, lat_t, w_uv_t, g_out)


def _mla_sample_kernel(q_ref, lat_ref, kr_ref, knew_ref, wuvt_ref, g_ref, o_ref, m_ref, acc_ref,
                       *, heads, tq, tk, c_dim, rope_dim):
    q = q_ref[...].reshape(heads * tq, q_ref.shape[2])
    q_lat = q[:, :c_dim]
    q_rope = q[:, c_dim:c_dim + rope_dim]
    _mla_init(m_ref, acc_ref)
    n_blk = lat_ref.shape[0] // tk

    def latent(kb):
        return lat_ref[pl.ds(pl.multiple_of(kb * tk, tk), tk), :].astype(BF16)

    def scores(kb):
        kr = kr_ref[pl.ds(pl.multiple_of(kb * tk, tk), tk), :].astype(BF16)
        return _mla_scores([q_lat, q_rope], [latent(kb), kr])

    def past_block(kb, s):
        s_next = scores(kb + 1)
        _mla_update(s, latent(kb), m_ref, acc_ref, None, False)
        return s_next

    s = lax.fori_loop(0, n_blk - 1, past_block, scores(0))
    knew = knew_ref[...]
    s_new = _mla_scores([q], [knew])
    _mla_update(s, latent(n_blk - 1), m_ref, acc_ref, None, False)
    _mla_update(s_new, knew[:, :c_dim], m_ref, acc_ref, None, False)
    _mla_finish(acc_ref, wuvt_ref, g_ref, o_ref, heads, tq)


def _mla_sample(qcat, cache_lat, cache_kr, kcat, w_uv_t, g_out, *, tk):
    heads, t, dk = qcat.shape
    batch, n_past, c_dim = cache_lat.shape
    rope_dim = cache_kr.shape[2]
    tq = qh = t // batch
    v_dim = w_uv_t.shape[1]
    assert n_past % CHUNK == 0 and qh <= CHUNK and n_past % tk == 0 and (heads * tq) % LANES == 0
    return pl.pallas_call(
        functools.partial(_mla_sample_kernel, heads=heads, tq=tq, tk=tk, c_dim=c_dim, rope_dim=rope_dim),
        grid=(batch,),
        in_specs=[
            pl.BlockSpec((heads, qh, dk), lambda b: (0, b, 0)),
            pl.BlockSpec((None, n_past, c_dim), lambda b: (b, 0, 0)),
            pl.BlockSpec((None, n_past, rope_dim), lambda b: (b, 0, 0)),
            pl.BlockSpec((qh, dk), lambda b: (b, 0)),
            pl.BlockSpec(w_uv_t.shape, lambda b: (0, 0, 0)),
            pl.BlockSpec(g_out.shape, lambda b: (0, 0, 0)),
        ],
        out_specs=pl.BlockSpec((qh, heads * v_dim), lambda b: (b, 0)),
        out_shape=jax.ShapeDtypeStruct((t, heads * v_dim), BF16),
        scratch_shapes=_mla_scratch(heads * tq, c_dim),
        compiler_params=_params("parallel"),
        name="mla_sample",
    )(qcat, cache_lat, cache_kr, kcat, w_uv_t, g_out)


def _sb_update(q2, kblk, vblk, triu, carry, acc, before, skip_dead=False):
    z = _dot_nt(kblk, q2)
    sub = triu.shape[0]

    def live():
        sp = jnp.maximum(z, 0.0) + jnp.log(1.0 + jnp.exp(-jnp.abs(z)))
        if before is not None:
            sp = jnp.where(before, sp, 0.0)
        hi = sp.astype(BF16)
        lo = (sp - hi.astype(F32)).astype(BF16)
        cs, run = [], carry
        for r0 in reversed(range(0, z.shape[0], sub)):
            cs.insert(0, _dot(triu, hi[r0:r0 + sub]) + _dot(triu, lo[r0:r0 + sub]) + run)
            run = cs[0][0:1, :]
        c = cs[0] if len(cs) == 1 else jnp.concatenate(cs, axis=0)
        a = jnp.exp(z - c)
        if before is not None:
            a = jnp.where(before, a, 0.0)
        return run, acc + _dot_tn(vblk() if callable(vblk) else vblk, a.astype(BF16))

    if not skip_dead:
        return live()
    return lax.cond(jnp.max(z - carry) > DEAD_EXPONENT, live, lambda: (carry, acc))


def _sb_queries(q, head_dim):
    lane = lax.broadcasted_iota(jnp.int32, (1, q.shape[1]), 1) // head_dim
    return jnp.concatenate([jnp.where(lane == h, q, jnp.zeros_like(q)) for h in range(q.shape[1] // head_dim)], axis=0)


def _sb_query_norms(q2):
    q2f = q2.astype(F32)
    return jnp.sqrt(_dot_nt(jnp.ones((SUBLANES, q2.shape[1]), BF16), (q2f * q2f).astype(BF16))[0:1, :]) * NORM_MARGIN


def _sb_before(tk, tq, n, first_query_row=0):
    key = lax.broadcasted_iota(jnp.int32, (tk, n), 0) - first_query_row
    qry = lax.broadcasted_iota(jnp.int32, (tk, n), 1) % tq
    return key < qry


def _sb_finish(acc, g_ref, o_ref, head_dim, tq):
    if tq % LANES == 0:
        outs = []
        for h in range(acc.shape[1] // tq):
            rows = slice(h * head_dim, (h + 1) * head_dim)
            o = acc[rows, h * tq:(h + 1) * tq]
            outs.append(o * lax.rsqrt(jnp.sum(o * o, axis=0, keepdims=True) / head_dim + EPS) * g_ref[rows, :])
        o_ref[...] = jnp.concatenate(outs, axis=0).T.astype(BF16)
    else:
        o_ref[...] = _own_block_rms_t(acc, g_ref[...], head_dim, tq).astype(BF16)


def _sb_prompt_kernel(kmax_ref, q_ref, k_ref, v_ref, tri_ref, g_ref, o_ref, *, tq, head_dim):
    b, p, j = pl.program_id(0), pl.program_id(1), pl.program_id(2)
    n_blk = pl.num_programs(2)
    q2 = _sb_queries(q_ref[...], head_dim)
    width = q_ref.shape[1]
    n = q2.shape[0]
    hpg = width // head_dim
    triu = tri_ref[...]
    zero = (jnp.zeros((1, n), F32), jnp.zeros((width, n), F32))

    def own_block():
        return _sb_update(q2, k_ref[pl.ds(0, tq), :], v_ref[pl.ds(0, tq), :], triu, *zero, _sb_before(tq, tq, n))

    def own_and_previous_block():
        k0 = pl.multiple_of((j - 1) * tq, tq)
        return _sb_update(q2, k_ref[pl.ds(k0, 2 * tq), :], v_ref[pl.ds(k0, 2 * tq), :], triu, *zero,
                          _sb_before(2 * tq, tq, n, first_query_row=tq))

    carry, acc = lax.cond(j > 0, own_and_previous_block, own_block)
    qn = _sb_query_norms(q2)
    lane_head = lax.broadcasted_iota(jnp.int32, (1, n), 1) // tq

    def alive(kb, carry):
        base = (b * n_blk + jnp.maximum(kb, 0)) * (hpg * pl.num_programs(1)) + p * hpg
        kmax = jnp.zeros((1, n), F32)
        for h in range(hpg):
            kmax = jnp.where(lane_head == h, kmax_ref[base + h], kmax)
        return jnp.logical_and(kb >= 0, jnp.max(qn * kmax - carry) > DEAD_EXPONENT)

    def cond(state):
        return state[1]

    def earlier_block(state):
        kb, _, carry, acc = state
        k0 = pl.multiple_of(kb * tq, tq)
        carry, acc = _sb_update(q2, k_ref[pl.ds(k0, tq), :], v_ref[pl.ds(k0, tq), :], triu, carry, acc, None)
        return kb - 1, alive(kb - 1, carry), carry, acc

    state = lax.while_loop(cond, earlier_block, (j - 2, alive(j - 2, carry), carry, acc))
    _sb_finish(state[3], g_ref, o_ref, head_dim, tq)


def _triu(n):
    r = lax.broadcasted_iota(jnp.int32, (n, n), 0)
    c = lax.broadcasted_iota(jnp.int32, (n, n), 1)
    return (c >= r).astype(BF16)


def _sb_prompt(q_sb, k_sb, v_sb, ksq, g_col, *, batch, seq, tq, head_dim, gw):
    t, sbw = q_sb.shape
    nq = seq // tq
    n_grp = sbw // gw
    n_heads = sbw // head_dim
    blk_max = jnp.max(ksq[:, :n_heads].reshape(batch, nq, tq, n_heads), axis=2)
    kmax = (jnp.sqrt(lax.cummax(blk_max, axis=1)) * NORM_MARGIN).reshape(-1)
    return pl.pallas_call(
        functools.partial(_sb_prompt_kernel, tq=tq, head_dim=head_dim),
        grid_spec=pltpu.PrefetchScalarGridSpec(
            num_scalar_prefetch=1,
            grid=(batch, n_grp, nq),
            in_specs=[
                pl.BlockSpec((tq, gw), lambda b, p, j, kmax: (b * nq + j, p)),
                pl.BlockSpec((seq, gw), lambda b, p, j, kmax: (b, p)),
                pl.BlockSpec((seq, gw), lambda b, p, j, kmax: (b, p)),
                pl.BlockSpec((tq, tq), lambda b, p, j, kmax: (0, 0)),
                pl.BlockSpec((gw, 1), lambda b, p, j, kmax: (p, 0)),
            ],
            out_specs=pl.BlockSpec((tq, gw), lambda b, p, j, kmax: (b * nq + j, p)),
        ),
        out_shape=jax.ShapeDtypeStruct((t, sbw), BF16),
        compiler_params=_params("parallel", "parallel", "arbitrary"),
        name="sb_prompt",
    )(kmax, q_sb, k_sb, v_sb, _triu(tq), g_col)


def _sb_sample_kernel(q_ref, kc_ref, vc_ref, kn_ref, vn_ref, tri_ref, g_ref, o_ref, *, tk, head_dim):
    tq, width = q_ref.shape
    q2 = _sb_queries(q_ref[...], head_dim)
    n = q2.shape[0]
    r = lax.broadcasted_iota(jnp.int32, (tq, tq), 0)
    c = lax.broadcasted_iota(jnp.int32, (tq, tq), 1)
    state = _sb_update(q2, kn_ref[...], vn_ref[...], (c >= r).astype(BF16),
                       jnp.zeros((1, n), F32), jnp.zeros((width, n), F32), _sb_before(tq, tq, n))
    triu = tri_ref[...]
    n_blk = kc_ref.shape[0] // tk

    def cached_block(kb, state, skip_dead):
        k0 = pl.multiple_of(kb * tk, tk)
        return _sb_update(q2, kc_ref[pl.ds(k0, tk), :].astype(BF16), lambda: vc_ref[pl.ds(k0, tk), :].astype(BF16),
                          triu, state[0], state[1], None, skip_dead=skip_dead)

    state = cached_block(n_blk - 1, state, False)
    rest = (n_blk - 1) * tk
    z_rest = _dot_nt(kc_ref[:rest, :].astype(BF16), q2)

    def walk():
        return lax.fori_loop(0, n_blk - 1, lambda i, st: cached_block(n_blk - 2 - i, st, True), state)

    state = lax.cond(jnp.max(z_rest - state[0]) > DEAD_EXPONENT, walk, lambda: state)
    _sb_finish(state[1], g_ref, o_ref, head_dim, tq)


def _sb_sample_full(q_sb, cache_k, cache_v, k_sb, v_sb, g_col, *, tk, head_dim):
    t, sbw = q_sb.shape
    batch, n_past, _ = cache_k.shape
    tq = t // batch
    n_grp = sbw // LANES
    assert n_past % tk == 0
    new = pl.BlockSpec((tq, LANES), lambda b, p: (b, p))
    past = pl.BlockSpec((None, n_past, LANES), lambda b, p: (b, 0, p))
    return pl.pallas_call(
        functools.partial(_sb_sample_kernel, tk=tk, head_dim=head_dim),
        grid=(batch, n_grp),
        in_specs=[new, past, past, new, new,
                  pl.BlockSpec((tk, tk), lambda b, p: (0, 0)),
                  pl.BlockSpec((LANES, 1), lambda b, p: (p, 0))],
        out_specs=new,
        out_shape=jax.ShapeDtypeStruct((t, sbw), BF16),
        compiler_params=_params("parallel", "parallel"),
        name="sb_sample_full",
    )(q_sb, cache_k, cache_v, k_sb, v_sb, _triu(tk), g_col)


def _sb_sample_recent_kernel(q_ref, kc_ref, vl_ref, kn_ref, vn_ref, tri_ref, g_ref, o_ref, bound_ref, *, head_dim):
    tq, width = q_ref.shape
    tk = vl_ref.shape[0]
    rest = kc_ref.shape[0] - tk
    q2 = _sb_queries(q_ref[...], head_dim)
    n = q2.shape[0]
    r = lax.broadcasted_iota(jnp.int32, (tq, tq), 0)
    c = lax.broadcasted_iota(jnp.int32, (tq, tq), 1)
    state = _sb_update(q2, kn_ref[...], vn_ref[...], (c >= r).astype(BF16),
                       jnp.zeros((1, n), F32), jnp.zeros((width, n), F32), _sb_before(tq, tq, n))
    carry, acc = _sb_update(q2, kc_ref[rest:, :].astype(BF16), vl_ref[...].astype(BF16), tri_ref[...], *state, None)
    z_rest = _dot_nt(kc_ref[:rest, :].astype(BF16), q2)
    bound = jnp.max(jnp.max(z_rest - carry, axis=0, keepdims=True), axis=1, keepdims=True)
    bound_ref[...] = jnp.broadcast_to(bound, bound_ref.shape)
    _sb_finish(acc, g_ref, o_ref, head_dim, tq)


def _sb_sample(q_sb, cache_k, cache_v, k_sb, v_sb, g_col, *, tk, head_dim):
    t, sbw = q_sb.shape
    batch, n_past = cache_k.shape[:2]
    tq = t // batch
    n_grp = sbw // LANES
    assert n_past % tk == 0
    keys = cache_k.reshape(batch, n_past, sbw)
    v_last = cache_v[:, n_past - tk:].reshape(batch, tk, sbw)
    new = pl.BlockSpec((tq, LANES), lambda b, p: (b, p))
    o_fast, bound = pl.pallas_call(
        functools.partial(_sb_sample_recent_kernel, head_dim=head_dim),
        grid=(batch, n_grp),
        in_specs=[new,
                  pl.BlockSpec((None, n_past, LANES), lambda b, p: (b, 0, p)),
                  pl.BlockSpec((None, tk, LANES), lambda b, p: (b, 0, p)),
                  new, new,
                  pl.BlockSpec((tk, tk), lambda b, p: (0, 0)),
                  pl.BlockSpec((LANES, 1), lambda b, p: (p, 0))],
        out_specs=(new, pl.BlockSpec((None, None, SUBLANES, LANES), lambda b, p: (b, p, 0, 0))),
        out_shape=(jax.ShapeDtypeStruct((t, sbw), BF16),
                   jax.ShapeDtypeStruct((batch, n_grp, SUBLANES, LANES), F32)),
        compiler_params=_params("parallel", "parallel"),
        name="sb_sample",
    )(q_sb, keys, v_last, k_sb, v_sb, _triu(tk), g_col)

    def full():
        return _sb_sample_full(q_sb, keys, cache_v.reshape(batch, n_past, sbw), k_sb, v_sb, g_col,
                               tk=tk, head_dim=head_dim)

    return lax.cond(jnp.max(bound) > DEAD_EXPONENT, full, lambda: o_fast)


def _rot_cols(w):
    half = w.shape[-1] // 2
    return jnp.concatenate([-w[..., half:], w[..., :half]], axis=-1)


def _rope_tables(pos, rope_dim, rows):
    half = rope_dim // 2
    inv_freq = ROPE_THETA ** (-jnp.arange(half, dtype=F32) / half)
    ang = pos.astype(F32)[:, None] * inv_freq[None, :]
    pad = jnp.zeros((pos.shape[0], LANES - rope_dim), F32)
    cosp = jnp.concatenate([jnp.cos(ang), jnp.cos(ang), pad], axis=1)
    sinp = jnp.concatenate([jnp.sin(ang), jnp.sin(ang), pad], axis=1)
    reps = max(1, rows // pos.shape[0])
    return jnp.tile(cosp, (reps, 1)), jnp.tile(sinp, (reps, 1))


def kernel(x_prompt, x_sample, cache_mla_latent, cache_mla_krope, cache_sb_k, cache_sb_v, g_pre_ff1, w_gate1, w_up1, w_down1, g_post_ff1, g_pre_mix, w_in, g_q, w_uq, g_kv, w_uk, w_uv, g_mla_out, g_sb_out, w_out, g_post_mix, g_pre_ff2, w_gate2, w_up2, w_down2, g_post_ff2, g_final):
    depth = w_in.shape[0]
    assert depth == 1
    batch, seq, d = x_prompt.shape
    dec_batch, dec_seq, _ = x_sample.shape
    n_past = cache_mla_latent.shape[2]
    _, q_lora, heads, qk_dim = w_uq.shape
    kv_lora, _, nope = w_uk.shape[1:]
    rope_dim = qk_dim - nope
    v_dim = w_uv.shape[3]
    sb_heads, sb_dim = g_sb_out.shape[1:]
    sbw = sb_heads * sb_dim
    mla_scale = float(qk_dim) ** -0.5
    sb_scale = float(sb_dim) ** -0.5
    assert kv_lora == LANES and 2 * rope_dim <= LANES and LANES % sb_dim == 0
    dims = (heads, q_lora, kv_lora, rope_dim, sbw, sb_heads, mla_scale * LOG2E, sb_scale)

    row = lambda g: g.reshape(1, -1).astype(F32)
    l = 0
    wi = w_in[l]
    i1 = q_lora + kv_lora
    i2 = i1 + rope_dim
    w_kr = wi[:, i1:i2]
    w_in_ext = jnp.concatenate(
        [wi[:, :i1], w_kr, _rot_cols(w_kr), jnp.zeros((d, LANES - 2 * rope_dim), F32), wi[:, i2:]], axis=1).astype(BF16)
    wq = w_uq[l]
    wq_rope = wq[:, :, nope:]
    wq_grp = jnp.concatenate(
        [wq_rope, _rot_cols(wq_rope), jnp.zeros((q_lora, heads, LANES - 2 * rope_dim), F32)], axis=2)
    w_uq_ext = jnp.concatenate(
        [wq[:, :, :nope].reshape(q_lora, heads * nope), wq_grp.reshape(q_lora, heads * LANES)], axis=1).astype(BF16)
    wk = jnp.transpose(w_uk[l], (1, 2, 0))
    eye = jnp.eye(heads, dtype=F32)
    w_uk_bd = (wk[:, :, None, :] * eye[:, None, :, None]).reshape(heads * nope, heads * kv_lora).astype(BF16)
    w_uv_t = jnp.transpose(w_uv[l], (1, 2, 0)).astype(BF16)
    w_out_a = w_out[l][:heads * v_dim].astype(BF16)
    w_out_b = w_out[l][heads * v_dim:].astype(BF16)
    ffn1 = (row(g_pre_ff1[l]), w_gate1[l].astype(BF16), w_up1[l].astype(BF16), w_down1[l].astype(BF16),
            row(g_post_ff1[l]), row(g_final[l]))
    ffn2 = (row(g_pre_ff2[l]), w_gate2[l].astype(BF16), w_up2[l].astype(BF16), w_down2[l].astype(BF16),
            row(g_post_ff2[l]), row(g_final[l]))
    g_mla = g_mla_out[l].reshape(heads, v_dim, 1).astype(F32)
    g_sb = g_sb_out[l].reshape(sbw, 1).astype(F32)

    def layer(x, pos, tm, attend):
        h = _ffn(x, *ffn1, final=False, tm=tm)
        cosp, sinp = _rope_tables(pos, rope_dim, tm)
        lat, kr, kn, vn, qcat, kcat, lat_t, q_sb, k_sb, v_sb, ksq = _mix_in(
            h, row(g_pre_mix[l]), w_in_ext, row(g_q[l]), w_uq_ext, w_uk_bd, row(g_kv[l]), cosp, sinp,
            tm=tm, dims=dims)
        o_mla, o_sb = attend(qcat, kcat, lat_t, q_sb, k_sb, v_sb, ksq)
        y = _ffn(h, *ffn2, final=True, tm=tm, mix=(o_mla, o_sb, w_out_a, w_out_b, row(g_post_mix[l])))
        return y, (lat, kr, kn, vn)

    def attend_prompt(qcat, kcat, lat_t, q_sb, k_sb, v_sb, ksq):
        o_mla = _mla_prompt(qcat, kcat, lat_t, w_uv_t, g_mla, batch=batch, seq=seq, tq=512, tk=512)
        o_sb = _sb_prompt(q_sb, k_sb, v_sb, ksq, g_sb, batch=batch, seq=seq, tq=256, head_dim=sb_dim, gw=2 * LANES)
        return o_mla, o_sb

    def attend_sample(qcat, kcat, lat_t, q_sb, k_sb, v_sb, ksq):
        o_mla = _mla_sample(qcat, cache_mla_latent[l], cache_mla_krope[l], kcat, w_uv_t, g_mla, tk=512)
        o_sb = _sb_sample(q_sb, cache_sb_k[l], cache_sb_v[l], k_sb, v_sb, g_sb, tk=256, head_dim=sb_dim)
        return o_mla, o_sb

    pos_p = jnp.arange(seq, dtype=jnp.int32)
    pos_s = n_past + jnp.arange(dec_seq, dtype=jnp.int32)
    yp, rp = layer(x_prompt.reshape(batch * seq, d), pos_p, 512, attend_prompt)
    ys, rs = layer(x_sample.reshape(dec_batch * dec_seq, d), pos_s, 512, attend_sample)

    def rows(r, b, s):
        lat, kr, kn, vn = r
        return (lat.reshape(1, b, s, kv_lora), kr.reshape(1, b, s, rope_dim),
                kn.reshape(1, b, s, sb_heads, sb_dim), vn.reshape(1, b, s, sb_heads, sb_dim))

    return (yp.reshape(batch, seq, d), ys.reshape(dec_batch, dec_seq, d)) + rows(rp, batch, seq) + rows(rs, dec_batch, dec_seq)
```
